```python
import math
import jax, jax.numpy as jnp
from jax import lax
import numpy as np

D_MODEL = 1024
BATCH = 2
SEQ = 8192
DEPTH = 2

N_META = 16
Q_BLOCK = 128
EPS = 1e-6

MLA_HEADS = 8
MLA_Q_RANK = 256
MLA_KV_RANK = 128
MLA_NOPE = 64
MLA_ROPE = 32
MLA_V = 64
ROPE_BASE = 10000.0
MLA_WIDTH = MLA_HEADS * MLA_V

DIFF_HEADS = 4
DIFF_HD = 64
DIFF_V = 2 * DIFF_HD
DIFF_WIDTH = DIFF_HEADS * DIFF_V
DIFF_QK = DIFF_HEADS * 2 * DIFF_HD

D_FF = 4 * D_MODEL

IN_SPLITS = (MLA_Q_RANK, MLA_KV_RANK, MLA_ROPE, DIFF_QK, DIFF_QK, DIFF_WIDTH, D_MODEL, D_MODEL)
D_IN = sum(IN_SPLITS)
SPLIT_IDX = tuple(sum(IN_SPLITS[:i + 1]) for i in range(len(IN_SPLITS) - 1))

kernel_name = "hybrid_mla_diffattn_gated_sqrelu"


def rmsnorm(x, g):
    xf = x.astype(jnp.float32)
    y = xf * lax.rsqrt(jnp.mean(xf * xf, axis=-1, keepdims=True) + EPS)
    return (y * g.astype(jnp.float32)).astype(x.dtype)


def rope_tables(L):
    inv = 1.0 / (ROPE_BASE ** (jnp.arange(0, MLA_ROPE, 2, dtype=jnp.float32) / MLA_ROPE))
    ang = jnp.arange(L, dtype=jnp.float32)[:, None] * inv[None, :]
    return jnp.cos(ang), jnp.sin(ang)


def apply_rope(x, cos, sin):
    x1, x2 = jnp.split(x.astype(jnp.float32), 2, axis=-1)
    return jnp.concatenate([x1 * cos - x2 * sin, x1 * sin + x2 * cos], axis=-1).astype(x.dtype)


def alibi_slopes():
    i = jnp.arange(1, DIFF_HEADS + 1, dtype=jnp.float32)
    return 2.0 ** (-8.0 * i / DIFF_HEADS)


def sweep_queries(attend, qs):
    L = qs[0].shape[1]
    n_real = L - N_META
    n_blk = n_real // Q_BLOCK
    meta_out = attend(tuple(q[:, :N_META] for q in qs), jnp.arange(N_META))

    def to_blocks(q):
        r = q[:, N_META:].reshape((q.shape[0], n_blk, Q_BLOCK) + q.shape[2:])
        return jnp.moveaxis(r, 1, 0)

    pos = (N_META + jnp.arange(n_real)).reshape(n_blk, Q_BLOCK)
    blk_out = lax.map(lambda a: attend(a[0], a[1]), (tuple(to_blocks(q) for q in qs), pos))
    real_out = jnp.moveaxis(blk_out, 0, 1)
    real_out = real_out.reshape((real_out.shape[0], n_real) + real_out.shape[3:])
    return jnp.concatenate([meta_out, real_out], axis=1)


def mla_attention(q_nope, q_rope, k_nope, k_rope, v):
    k_pos = jnp.arange(k_nope.shape[1])
    scale = (MLA_NOPE + MLA_ROPE) ** -0.5

    def attend(qb, q_pos):
        qn, qr = qb
        s = jnp.einsum('bqhd,bkhd->bhqk', qn, k_nope) + jnp.einsum('bqhr,bkr->bhqk', qr, k_rope)
        s = s.astype(jnp.float32) * scale
        s = jnp.where(k_pos[None, :] <= q_pos[:, None], s, -jnp.inf)
        p = jax.nn.softmax(s, axis=-1).astype(v.dtype)
        return jnp.einsum('bhqk,bkhd->bqhd', p, v)

    return sweep_queries(attend, (q_nope, q_rope))


def diff_attention(q, k, v, lam, slopes):
    k_pos = jnp.arange(k.shape[1])
    scale = DIFF_HD ** -0.5

    def attend(qb, q_pos):
        (qq,) = qb
        s = jnp.einsum('bqhjd,bkhjd->bhjqk', qq, k).astype(jnp.float32) * scale
        dist = (q_pos[:, None] - k_pos[None, :]).astype(jnp.float32)
        s = s - slopes[None, :, None, None, None] * dist
        s = jnp.where(dist >= 0, s, -jnp.inf)
        p = jax.nn.softmax(s, axis=-1)
        a = (p[:, :, 0] - lam * p[:, :, 1]).astype(v.dtype)
        return jnp.einsum('bhqk,bkhe->bqhe', a, v)

    return sweep_queries(attend, (q,))


def setup_inputs(seed: int = 0) -> dict:
    key = jax.random.key(seed)
    ks = jax.random.split(key, 24)

    def dense(k, shape):
        return jax.random.normal(k, shape, jnp.float32) * (shape[-2] ** -0.5)

    def gain(k, shape):
        return 1.0 + 0.05 * jax.random.normal(k, shape, jnp.float32)

    return {
        "x": jax.random.normal(ks[0], (BATCH, SEQ, D_MODEL), jnp.float32),
        "meta_tokens": jax.random.normal(ks[1], (N_META, D_MODEL), jnp.float32),
        "attn_norm": gain(ks[2], (DEPTH, D_MODEL)),
        "w_in": dense(ks[3], (DEPTH, D_MODEL, D_IN)),
        "b_gate": 0.1 * jax.random.normal(ks[4], (DEPTH, 2, D_MODEL), jnp.float32),
        "mla_q_norm": gain(ks[5], (DEPTH, MLA_Q_RANK)),
        "w_q_up": dense(ks[6], (DEPTH, MLA_Q_RANK, MLA_HEADS * (MLA_NOPE + MLA_ROPE))),
        "mla_kv_norm": gain(ks[7], (DEPTH, MLA_KV_RANK)),
        "w_kv_up": dense(ks[8], (DEPTH, MLA_KV_RANK, MLA_HEADS * (MLA_NOPE + MLA_V))),
        "lambda_q1": 0.1 * jax.random.normal(ks[9], (DEPTH, DIFF_HD), jnp.float32),
        "lambda_k1": 0.1 * jax.random.normal(ks[10], (DEPTH, DIFF_HD), jnp.float32),
        "lambda_q2": 0.1 * jax.random.normal(ks[11], (DEPTH, DIFF_HD), jnp.float32),
        "lambda_k2": 0.1 * jax.random.normal(ks[12], (DEPTH, DIFF_HD), jnp.float32),
        "diff_subln": gain(ks[13], (DEPTH, DIFF_V)),
        "w_a_proj": dense(ks[14], (DEPTH, MLA_WIDTH, D_MODEL)),
        "w_b_proj": dense(ks[15], (DEPTH, DIFF_WIDTH, D_MODEL)),
        "w_o": dense(ks[16], (DEPTH, D_MODEL, D_MODEL)),
        "mlp_norm": gain(ks[17], (DEPTH, D_MODEL)),
        "w_up": dense(ks[18], (DEPTH, D_MODEL, D_FF)),
        "w_down": dense(ks[19], (DEPTH, D_FF, D_MODEL)),
        "final_norm": gain(ks[20], (D_MODEL,)),
    }


def reference(x, meta_tokens, attn_norm, w_in, b_gate, mla_q_norm, w_q_up, mla_kv_norm,
              w_kv_up, lambda_q1, lambda_k1, lambda_q2, lambda_k2, diff_subln,
              w_a_proj, w_b_proj, w_o, mlp_norm, w_up, w_down, final_norm):
    B = x.shape[0]
    meta = jnp.broadcast_to(meta_tokens[None].astype(x.dtype), (B, N_META, D_MODEL))
    x = jnp.concatenate([meta, x], axis=1)
    L = x.shape[1]
    cos, sin = rope_tables(L)
    slopes = alibi_slopes()

    for l in range(DEPTH):
        h = rmsnorm(x, attn_norm[l])
        proj = h @ w_in[l]
        q_a, kv_a, k_r, dq, dk, dv, g_a, g_b = jnp.split(proj, SPLIT_IDX, axis=-1)

        q = (rmsnorm(q_a, mla_q_norm[l]) @ w_q_up[l]).reshape(B, L, MLA_HEADS, MLA_NOPE + MLA_ROPE)
        q_nope = q[..., :MLA_NOPE]
        q_rope = apply_rope(q[..., MLA_NOPE:], cos[None, :, None], sin[None, :, None])
        kv = (rmsnorm(kv_a, mla_kv_norm[l]) @ w_kv_up[l]).reshape(B, L, MLA_HEADS, MLA_NOPE + MLA_V)
        k_nope = kv[..., :MLA_NOPE]
        v_mla = kv[..., MLA_NOPE:]
        k_rope = apply_rope(k_r, cos[None], sin[None])
        a_out = mla_attention(q_nope, q_rope, k_nope, k_rope, v_mla).reshape(B, L, MLA_WIDTH)

        lam_init = 0.8 - 0.6 * math.exp(-0.3 * l)
        lam = (jnp.exp(jnp.sum((lambda_q1[l] * lambda_k1[l]).astype(jnp.float32)))
               - jnp.exp(jnp.sum((lambda_q2[l] * lambda_k2[l]).astype(jnp.float32)))
               + lam_init)
        d_out = diff_attention(dq.reshape(B, L, DIFF_HEADS, 2, DIFF_HD),
                               dk.reshape(B, L, DIFF_HEADS, 2, DIFF_HD),
                               dv.reshape(B, L, DIFF_HEADS, DIFF_V), lam, slopes)
        d_out = (rmsnorm(d_out, diff_subln[l]) * (1.0 - lam_init)).reshape(B, L, DIFF_WIDTH)

        gate_a = jax.nn.sigmoid(g_a + b_gate[l, 0])
        gate_b = jax.nn.sigmoid(g_b + b_gate[l, 1])
        y = gate_a * (a_out @ w_a_proj[l]) + gate_b * (d_out @ w_b_proj[l])
        x = x + y @ w_o[l]

        h = rmsnorm(x, mlp_norm[l])
        x = x + jnp.square(jax.nn.relu(h @ w_up[l])) @ w_down[l]

    x = rmsnorm(x, final_norm)
    return x[:, N_META:]
```

```python
import functools
import math

import jax
import jax.numpy as jnp
from jax import lax
from jax.experimental import pallas as pl
from jax.experimental.pallas import tpu as pltpu

D_MODEL = 1024
N_META = 16
EPS = 1e-6
MLA_HEADS = 8
MLA_Q_RANK = 256
MLA_KV_RANK = 128
MLA_NOPE = 64
MLA_ROPE = 32
MLA_V = 64
ROPE_BASE = 10000.0
DIFF_HEADS = 4
DIFF_HD = 64
DIFF_V = 128
D_FF = 4 * D_MODEL

LOG2E = 1.4426950408889634
NEG = -1e30

CH = 256
TM = 512
SLOT = 128
MLA_VROWS = 80
DIFF_VROWS = 144
VMEM_LIMIT = 56 * 1024 * 1024

F32 = jnp.float32
BF16 = jnp.bfloat16


def _rms(x, g):
    return x * lax.rsqrt(jnp.mean(x * x, axis=-1, keepdims=True) + EPS) * g


def _dot(a, b):
    return jnp.dot(a, b, preferred_element_type=F32)


def _dot_nt(a, b):
    return lax.dot_general(a, b, (((1,), (1,)), ((), ())), preferred_element_type=F32)


def _dot_tn(a, b):
    return lax.dot_general(a, b, (((0,), (0,)), ((), ())), preferred_element_type=F32)


def _proj_kernel(x_ref, g_ref, w1_ref, gq_ref, gkv_ref, wqT_ref, wk_ref, wvT_ref, onev_ref,
                 cqT_ref, sqT_ref, ck_ref, sk_ref, wdqT_ref, wdk_ref, wdvT_ref, onedv_ref,
                 wg_ref, bg_ref,
                 qT_out, km_out, vT_out, dqT_out, dkm_out, dvT_out, gate_out):
    x = x_ref[...]
    hb = _rms(x, g_ref[...]).astype(BF16)

    p1 = _dot(hb, w1_ref[...])
    qlat = _rms(p1[:, :MLA_Q_RANK], gq_ref[...]).astype(BF16)
    kvlat = _rms(p1[:, MLA_Q_RANK:MLA_Q_RANK + MLA_KV_RANK], gkv_ref[...]).astype(BF16)
    kr = p1[:, 384:512]
    krsw = p1[:, 512:640]

    qT = _dot_nt(wqT_ref[...], qlat)
    cq = cqT_ref[...]
    sq = sqT_ref[...]
    half = MLA_HEADS * SLOT
    for h in range(MLA_HEADS):
        r = (qT[h * SLOT:(h + 1) * SLOT] * cq + qT[half + h * SLOT:half + (h + 1) * SLOT] * sq).astype(BF16)
        for c in range(TM // CH):
            qT_out[c, h * SLOT:(h + 1) * SLOT, :] = r[:, c * CH:(c + 1) * CH]

    krot = kr * ck_ref[...] + krsw * sk_ref[...]
    kn = _dot(kvlat, wk_ref[...])
    for h in range(MLA_HEADS):
        km_out[:, h * SLOT:(h + 1) * SLOT] = (kn[:, h * SLOT:(h + 1) * SLOT] + krot).astype(BF16)

    vT = (_dot_nt(wvT_ref[...], kvlat) + onev_ref[...]).astype(BF16)
    for c in range(TM // CH):
        vT_out[c] = vT[:, c * CH:(c + 1) * CH]

    dqT = (_dot_nt(wdqT_ref[...], hb) * (DIFF_HD ** -0.5 * LOG2E)).astype(BF16)
    for c in range(TM // CH):
        dqT_out[c] = dqT[:, c * CH:(c + 1) * CH]
    dkm_out[...] = _dot(hb, wdk_ref[...]).astype(BF16)
    dvT = (_dot_nt(wdvT_ref[...], hb) + onedv_ref[...]).astype(BF16)
    for c in range(TM // CH):
        dvT_out[c] = dvT[:, c * CH:(c + 1) * CH]

    gate_out[...] = jax.nn.sigmoid(_dot(hb, wg_ref[...]) + bg_ref[...]).astype(BF16)


def _const_spec(shape):
    nd = len(shape)
    return pl.BlockSpec(shape, lambda *_: (0,) * nd)


def _proj_call(xp, lw, tabs):
    R = xp.shape[0]
    nt = R // TM
    nch = R // CH
    per = TM // CH
    row = lambda w: pl.BlockSpec((TM, w), lambda t: (t, 0))
    colT = lambda f: pl.BlockSpec((f, TM), lambda t: (0, t))
    chunk = lambda f: pl.BlockSpec((per, f, CH), lambda t: (t, 0, 0))
    ins = [
        (xp, row(D_MODEL)),
        (lw["g_attn"], _const_spec((1, D_MODEL))),
        (lw["w1"], _const_spec(lw["w1"].shape)),
        (lw["gq"], _const_spec((1, MLA_Q_RANK))),
        (lw["gkv"], _const_spec((1, MLA_KV_RANK))),
        (lw["wqT"], _const_spec(lw["wqT"].shape)),
        (lw["wk"], _const_spec(lw["wk"].shape)),
        (lw["wvT"], _const_spec(lw["wvT"].shape)),
        (tabs["onev"], _const_spec(tabs["onev"].shape)),
        (tabs["cqT"], colT(SLOT)),
        (tabs["sqT"], colT(SLOT)),
        (tabs["ck"], row(SLOT)),
        (tabs["sk"], row(SLOT)),
        (lw["wdqT"], _const_spec(lw["wdqT"].shape)),
        (lw["wdk"], _const_spec(lw["wdk"].shape)),
        (lw["wdvT"], _const_spec(lw["wdvT"].shape)),
        (tabs["onedv"], _const_spec(tabs["onedv"].shape)),
        (lw["wg"], _const_spec(lw["wg"].shape)),
        (lw["bg"], _const_spec((1, 2 * D_MODEL))),
    ]
    out_shape = [
        jax.ShapeDtypeStruct((nch, MLA_HEADS * SLOT, CH), BF16),
        jax.ShapeDtypeStruct((R, MLA_HEADS * SLOT), BF16),
        jax.ShapeDtypeStruct((nch, MLA_HEADS * MLA_VROWS, CH), BF16),
        jax.ShapeDtypeStruct((nch, DIFF_HEADS * SLOT, CH), BF16),
        jax.ShapeDtypeStruct((R, DIFF_HEADS * SLOT), BF16),
        jax.ShapeDtypeStruct((nch, DIFF_HEADS * DIFF_VROWS, CH), BF16),
        jax.ShapeDtypeStruct((R, 2 * D_MODEL), BF16),
    ]
    out_specs = [
        chunk(MLA_HEADS * SLOT),
        row(MLA_HEADS * SLOT),
        chunk(MLA_HEADS * MLA_VROWS),
        chunk(DIFF_HEADS * SLOT),
        row(DIFF_HEADS * SLOT),
        chunk(DIFF_HEADS * DIFF_VROWS),
        row(2 * D_MODEL),
    ]
    return pl.pallas_call(
        _proj_kernel,
        grid=(nt,),
        in_specs=[s for _, s in ins],
        out_specs=out_specs,
        out_shape=out_shape,
        compiler_params=pltpu.CompilerParams(
            dimension_semantics=("arbitrary",), vmem_limit_bytes=VMEM_LIMIT),
        name="proj",
    )(*[a for a, _ in ins])


def _tile_masks(i, n_real_tiles):
    krow = lax.broadcasted_iota(jnp.int32, (CH, CH), 0)
    qlane = lax.broadcasted_iota(jnp.int32, (CH, CH), 1)
    is_meta = i == n_real_tiles
    big = 2 * CH
    meta_mask = (krow < N_META) & (krow <= qlane + jnp.where(is_meta, 0, big))
    diag_mask = krow + jnp.where(is_meta, big, 0) <= qlane
    return meta_mask, diag_mask


def _softmax_step(s, m, acc, v, mask=None):
    if mask is not None:
        s = jnp.where(mask, s, NEG)
    m_new = jnp.maximum(m, jnp.max(s, axis=0, keepdims=True))
    alpha = jnp.exp2(m - m_new)
    p = jnp.exp2(s - m_new).astype(BF16)
    acc = acc * alpha + _dot(v, p)
    return m_new, acc


def _mla_kernel(qT_ref, k_ref, kmeta_ref, vT_ref, vmeta_ref, o_ref, *, n_real_tiles):
    i = pl.program_id(2)
    is_meta = i == n_real_tiles
    n_full = jnp.where(is_meta, 0, i)
    j_diag = jnp.where(is_meta, 0, i)
    meta_mask, diag_mask = _tile_masks(i, n_real_tiles)
    heads = qT_ref.shape[0] // SLOT

    qs = [qT_ref[h * SLOT:(h + 1) * SLOT, :] for h in range(heads)]

    def k_tile(j, h):
        return k_ref[pl.ds(pl.multiple_of(j * CH, CH), CH), h * SLOT:(h + 1) * SLOT]

    def v_tile(j, h):
        return vT_ref[j, h * MLA_VROWS:(h + 1) * MLA_VROWS, :]

    carry = []
    for h in range(heads):
        s = _dot(kmeta_ref[:, h * SLOT:(h + 1) * SLOT], qs[h])
        m0 = jnp.full((1, CH), NEG, F32)
        acc0 = jnp.zeros((MLA_VROWS, CH), F32)
        carry.extend(_softmax_step(s, m0, acc0, vmeta_ref[h * MLA_VROWS:(h + 1) * MLA_VROWS, :], meta_mask))

    def body(j, carry):
        out = []
        for h in range(heads):
            s = _dot(k_tile(j, h), qs[h])
            out.extend(_softmax_step(s, carry[2 * h], carry[2 * h + 1], v_tile(j, h)))
        return tuple(out)

    carry = lax.fori_loop(0, n_full, body, tuple(carry))

    for h in range(heads):
        s = _dot(k_tile(j_diag, h), qs[h])
        _, acc = _softmax_step(s, carry[2 * h], carry[2 * h + 1], v_tile(j_diag, h), diag_mask)
        o = acc[:MLA_V] / acc[MLA_V:MLA_V + 1]
        o_ref[h * MLA_V:(h + 1) * MLA_V, :] = o.astype(o_ref.dtype)


def _mla_call(qT, km, vT, B, n_real_tiles):
    hp = 2
    ngrp = MLA_HEADS // hp
    nch = qT.shape[0]
    R = km.shape[0]
    seq = n_real_tiles * CH
    meta_chunk0 = B * n_real_tiles

    def q_idx(b, g, i):
        return (jnp.where(i == n_real_tiles, meta_chunk0 + b, b * n_real_tiles + i), g, 0)

    in_specs = [
        pl.BlockSpec((None, hp * SLOT, CH), q_idx),
        pl.BlockSpec((seq, hp * SLOT), lambda b, g, i: (b, g)),
        pl.BlockSpec((CH, hp * SLOT), lambda b, g, i: (meta_chunk0 + b, g)),
        pl.BlockSpec((n_real_tiles, hp * MLA_VROWS, CH), lambda b, g, i: (b, g, 0)),
        pl.BlockSpec((None, hp * MLA_VROWS, CH), lambda b, g, i: (meta_chunk0 + b, g, 0)),
    ]
    out_spec = pl.BlockSpec((None, hp * MLA_V, CH), q_idx)
    return pl.pallas_call(
        functools.partial(_mla_kernel, n_real_tiles=n_real_tiles),
        grid=(B, ngrp, n_real_tiles + 1),
        in_specs=in_specs,
        out_specs=out_spec,
        out_shape=jax.ShapeDtypeStruct((nch, MLA_HEADS * MLA_V, CH), BF16),
        compiler_params=pltpu.CompilerParams(
            dimension_semantics=("arbitrary", "arbitrary", "arbitrary"), vmem_limit_bytes=VMEM_LIMIT),
        name="mla_attn",
    )(qT, km, km, vT, vT)


def _diff_kernel(lam_ref, subln_ref, qT_ref, k_ref, kmeta_ref, al_ref, almeta_ref, vT_ref, vmeta_ref,
                 o_ref, *, n_real_tiles, lam_init):
    i = pl.program_id(2)
    is_meta = i == n_real_tiles
    n_full = jnp.where(is_meta, 0, i)
    j_diag = jnp.where(is_meta, 0, i)
    meta_mask, diag_mask = _tile_masks(i, n_real_tiles)

    q = qT_ref[...]
    rowi = lax.broadcasted_iota(jnp.int32, (SLOT, CH), 0)
    ext = jnp.where(rowi < 3, 1.0, 0.0).astype(BF16)
    zero = jnp.zeros_like(q)
    qs = [jnp.concatenate([jnp.where(rowi < DIFF_HD, q, zero), ext], axis=0),
          jnp.concatenate([jnp.where(rowi >= DIFF_HD, q, zero), ext], axis=0)]

    def k_tile(j):
        rows = pl.ds(pl.multiple_of(j * CH, CH), CH)
        return jnp.concatenate([k_ref[rows, :], al_ref[rows, :]], axis=1)

    kmeta = jnp.concatenate([kmeta_ref[...], almeta_ref[...]], axis=1)
    vmeta = vmeta_ref[...]
    carry = []
    for mp in range(2):
        s = _dot(kmeta, qs[mp])
        m0 = jnp.full((1, CH), NEG, F32)
        acc0 = jnp.zeros((DIFF_VROWS, CH), F32)
        carry.extend(_softmax_step(s, m0, acc0, vmeta, meta_mask))

    def body(j, carry):
        k = k_tile(j)
        v = vT_ref[j]
        out = []
        for mp in range(2):
            s = _dot(k, qs[mp])
            out.extend(_softmax_step(s, carry[2 * mp], carry[2 * mp + 1], v))
        return tuple(out)

    carry = lax.fori_loop(0, n_full, body, tuple(carry))

    k = k_tile(j_diag)
    v = vT_ref[j_diag]
    outs = []
    for mp in range(2):
        s = _dot(k, qs[mp])
        _, acc = _softmax_step(s, carry[2 * mp], carry[2 * mp + 1], v, diag_mask)
        outs.append(acc[:DIFF_V] / acc[DIFF_V:DIFF_V + 1])

    lp = lam_ref[...]
    lam = (jnp.exp(jnp.sum(lp[0:1] * lp[1:2], axis=-1, keepdims=True))
           - jnp.exp(jnp.sum(lp[2:3] * lp[3:4], axis=-1, keepdims=True)) + lam_init)
    a = outs[0] - lam * outs[1]
    a = a * lax.rsqrt(jnp.mean(a * a, axis=0, keepdims=True) + EPS) * subln_ref[...] * (1.0 - lam_init)
    o_ref[...] = a.astype(o_ref.dtype)


def _diff_call(lam_params, subln_col, dqT, dkm, alibi, dvT, B, n_real_tiles, lam_init):
    nch = dqT.shape[0]
    seq = n_real_tiles * CH
    meta_chunk0 = B * n_real_tiles

    def q_idx(b, h, i):
        return (jnp.where(i == n_real_tiles, meta_chunk0 + b, b * n_real_tiles + i), h, 0)

    in_specs = [
        pl.BlockSpec((4, DIFF_HD), lambda b, h, i: (0, 0)),
        pl.BlockSpec((DIFF_V, 1), lambda b, h, i: (0, 0)),
        pl.BlockSpec((None, SLOT, CH), q_idx),
        pl.BlockSpec((seq, SLOT), lambda b, h, i: (b, h)),
        pl.BlockSpec((CH, SLOT), lambda b, h, i: (meta_chunk0 + b, h)),
        pl.BlockSpec((None, seq, SLOT), lambda b, h, i: (h, 0, 0)),
        pl.BlockSpec((None, CH, SLOT), lambda b, h, i: (h, seq // CH, 0)),
        pl.BlockSpec((n_real_tiles, DIFF_VROWS, CH), lambda b, h, i: (b, h, 0)),
        pl.BlockSpec((None, DIFF_VROWS, CH), lambda b, h, i: (meta_chunk0 + b, h, 0)),
    ]
    out_spec = pl.BlockSpec((None, DIFF_V, CH), q_idx)
    return pl.pallas_call(
        functools.partial(_diff_kernel, n_real_tiles=n_real_tiles, lam_init=lam_init),
        grid=(B, DIFF_HEADS, n_real_tiles + 1),
        in_specs=in_specs,
        out_specs=out_spec,
        out_shape=jax.ShapeDtypeStruct((nch, DIFF_HEADS * DIFF_V, CH), BF16),
        compiler_params=pltpu.CompilerParams(
            dimension_semantics=("arbitrary", "arbitrary", "arbitrary"), vmem_limit_bytes=VMEM_LIMIT),
        name="diff_attn",
    )(lam_params, subln_col, dqT, dkm, dkm, alibi, alibi, dvT, dvT)


def _out_kernel(x_ref, aT_ref, dT_ref, gate_ref, wa_ref, wb_ref, wo_ref, gm_ref, wup_ref, wdn_ref,
                gf_ref, o_ref, *, final):
    per = TM // CH
    ya = jnp.concatenate([_dot_tn(aT_ref[c], wa_ref[...]) for c in range(per)], axis=0)
    yb = jnp.concatenate([_dot_tn(dT_ref[c], wb_ref[...]) for c in range(per)], axis=0)
    gate = gate_ref[...].astype(F32)
    y = gate[:, :D_MODEL] * ya + gate[:, D_MODEL:] * yb
    x1 = x_ref[...] + _dot(y.astype(BF16), wo_ref[...])
    h2 = _rms(x1, gm_ref[...]).astype(BF16)
    up = _dot(h2, wup_ref[...])
    act = jnp.square(jnp.maximum(up, 0.0)).astype(BF16)
    x2 = x1 + _dot(act, wdn_ref[...])
    if final:
        x2 = _rms(x2, gf_ref[...])
    o_ref[...] = x2


def _out_call(xp, aT, dT, gate, lw, g_final, n_tiles, final):
    per = TM // CH
    row = lambda w: pl.BlockSpec((TM, w), lambda t: (t, 0))
    chunk = lambda f: pl.BlockSpec((per, f, CH), lambda t: (t, 0, 0))
    in_specs = [
        row(D_MODEL), chunk(MLA_HEADS * MLA_V), chunk(DIFF_HEADS * DIFF_V), row(2 * D_MODEL),
        _const_spec(lw["wa"].shape), _const_spec(lw["wb"].shape), _const_spec(lw["wo"].shape),
        _const_spec((1, D_MODEL)), _const_spec(lw["wup"].shape), _const_spec(lw["wdn"].shape),
        _const_spec((1, D_MODEL)),
    ]
    return pl.pallas_call(
        functools.partial(_out_kernel, final=final),
        grid=(n_tiles,),
        in_specs=in_specs,
        out_specs=row(D_MODEL),
        out_shape=jax.ShapeDtypeStruct((n_tiles * TM, D_MODEL), F32),
        compiler_params=pltpu.CompilerParams(
            dimension_semantics=("arbitrary",), vmem_limit_bytes=VMEM_LIMIT),
        name="out_mlp",
    )(xp, aT, dT, gate, lw["wa"], lw["wb"], lw["wo"], lw["g_mlp"], lw["wup"], lw["wdn"], g_final)


def _top16(v):
    bits = lax.bitcast_convert_type(v, jnp.uint32) & jnp.uint32(0xFFFF0000)
    return lax.bitcast_convert_type(bits, F32)


def _tables(B, seq):
    pos_real = N_META + jnp.arange(seq, dtype=F32)
    pos_meta = jnp.arange(CH, dtype=F32)
    pos = jnp.concatenate([pos_real] * B + [pos_meta] * B)
    inv = 1.0 / (ROPE_BASE ** (jnp.arange(0, MLA_ROPE, 2, dtype=F32) / MLA_ROPE))
    ang = pos[:, None] * inv[None, :]
    cos, sin = jnp.cos(ang), jnp.sin(ang)
    R = pos.shape[0]
    ones = jnp.ones((R, MLA_NOPE), F32)
    zeros_n = jnp.zeros((R, MLA_NOPE), F32)
    pad = jnp.zeros((R, SLOT - MLA_NOPE - MLA_ROPE), F32)
    ck = jnp.concatenate([ones, cos, cos, pad], axis=1)
    sk = jnp.concatenate([zeros_n, -sin, sin, pad], axis=1)
    cq = (MLA_NOPE + MLA_ROPE) ** -0.5 * LOG2E
    tabs = {"ck": ck, "sk": sk, "cqT": (ck * cq).T, "sqT": (sk * cq).T}

    slopes = 2.0 ** (-8.0 * jnp.arange(1, DIFF_HEADS + 1, dtype=F32) / DIFF_HEADS)
    pos1 = jnp.concatenate([pos_real, pos_meta])
    c = (slopes[:, None] * LOG2E) * pos1[None, :]
    c_hi = _top16(c)
    c_mid = _top16(c - c_hi)
    c_lo = c - c_hi - c_mid
    al = jnp.stack([c_hi, c_mid, c_lo], axis=-1).astype(BF16)
    tabs["alibi"] = jnp.pad(al, ((0, 0), (0, 0), (0, SLOT - 3)))

    onev = (jnp.arange(MLA_HEADS * MLA_VROWS) % MLA_VROWS == MLA_V).astype(F32)[:, None]
    onedv = (jnp.arange(DIFF_HEADS * DIFF_VROWS) % DIFF_VROWS == DIFF_V).astype(F32)[:, None]
    tabs["onev"] = onev
    tabs["onedv"] = onedv
    return tabs


def _layer_weights(l, attn_norm, w_in, b_gate, mla_q_norm, w_q_up, mla_kv_norm, w_kv_up,
                   w_a_proj, w_b_proj, w_o, mlp_norm, w_up, w_down):
    wi = w_in[l]
    o = 0
    parts = []
    for n in (MLA_Q_RANK, MLA_KV_RANK, MLA_ROPE, 512, 512, 512, D_MODEL, D_MODEL):
        parts.append(wi[:, o:o + n])
        o += n
    w_qa, w_kva, w_kr, w_dq, w_dk, w_dv, w_ga, w_gb = parts
    zl = jnp.zeros((D_MODEL, MLA_NOPE), F32)
    zr = jnp.zeros((D_MODEL, SLOT - MLA_NOPE - MLA_ROPE), F32)
    hr = MLA_ROPE // 2
    kr_slot = jnp.concatenate([zl, w_kr, zr], axis=1)
    krsw_slot = jnp.concatenate([zl, w_kr[:, hr:], w_kr[:, :hr], zr], axis=1)
    w1 = jnp.concatenate([w_qa, w_kva, kr_slot, krsw_slot], axis=1).astype(BF16)

    wq = w_q_up[l].reshape(MLA_Q_RANK, MLA_HEADS, MLA_NOPE + MLA_ROPE)
    qn, qr = wq[..., :MLA_NOPE], wq[..., MLA_NOPE:]
    z32 = jnp.zeros((MLA_Q_RANK, MLA_HEADS, SLOT - MLA_NOPE - MLA_ROPE), F32)
    z64 = jnp.zeros((MLA_Q_RANK, MLA_HEADS, MLA_NOPE), F32)
    raw = jnp.concatenate([qn, qr, z32], axis=-1).reshape(MLA_Q_RANK, MLA_HEADS * SLOT)
    sw = jnp.concatenate([z64, qr[..., hr:], qr[..., :hr], z32], axis=-1).reshape(MLA_Q_RANK, MLA_HEADS * SLOT)
    wqT = jnp.concatenate([raw, sw], axis=1).T.astype(BF16)

    wkv = w_kv_up[l].reshape(MLA_KV_RANK, MLA_HEADS, MLA_NOPE + MLA_V)
    kn, vv = wkv[..., :MLA_NOPE], wkv[..., MLA_NOPE:]
    wk = jnp.concatenate([kn, jnp.zeros_like(kn)], axis=-1).reshape(MLA_KV_RANK, MLA_HEADS * SLOT).astype(BF16)
    wvT = jnp.pad(vv, ((0, 0), (0, 0), (0, MLA_VROWS - MLA_V))).reshape(
        MLA_KV_RANK, MLA_HEADS * MLA_VROWS).T.astype(BF16)

    wdvT = jnp.pad(w_dv.reshape(D_MODEL, DIFF_HEADS, DIFF_V), ((0, 0), (0, 0), (0, DIFF_VROWS - DIFF_V))
                   ).reshape(D_MODEL, DIFF_HEADS * DIFF_VROWS).T.astype(BF16)
    return {
        "g_attn": attn_norm[l][None, :], "w1": w1,
        "gq": mla_q_norm[l][None, :], "gkv": mla_kv_norm[l][None, :],
        "wqT": wqT, "wk": wk, "wvT": wvT,
        "wdqT": w_dq.T.astype(BF16), "wdk": w_dk.astype(BF16), "wdvT": wdvT,
        "wg": jnp.concatenate([w_ga, w_gb], axis=1).astype(BF16),
        "bg": jnp.concatenate([b_gate[l, 0], b_gate[l, 1]])[None, :],
        "wa": w_a_proj[l].astype(BF16), "wb": w_b_proj[l].astype(BF16), "wo": w_o[l].astype(BF16),
        "g_mlp": mlp_norm[l][None, :], "wup": w_up[l].astype(BF16), "wdn": w_down[l].astype(BF16),
    }


def kernel(x, meta_tokens, attn_norm, w_in, b_gate, mla_q_norm, w_q_up, mla_kv_norm, w_kv_up, lambda_q1, lambda_k1, lambda_q2, lambda_k2, diff_subln, w_a_proj, w_b_proj, w_o, mlp_norm, w_up, w_down, final_norm):
    B, seq, D = x.shape
    depth = attn_norm.shape[0]
    assert D == D_MODEL and seq % TM == 0 and (B * CH) % TM == 0 and meta_tokens.shape[0] == N_META
    n_real_tiles = seq // CH
    n_real_rows = B * seq

    meta_blk = jnp.pad(meta_tokens.astype(x.dtype), ((0, CH - N_META), (0, 0)))
    xp = jnp.concatenate([x.reshape(n_real_rows, D)] + [meta_blk] * B, axis=0)
    R = xp.shape[0]
    tabs = _tables(B, seq)
    g_final = final_norm[None, :]

    for l in range(depth):
        lw = _layer_weights(l, attn_norm, w_in, b_gate, mla_q_norm, w_q_up, mla_kv_norm, w_kv_up,
                            w_a_proj, w_b_proj, w_o, mlp_norm, w_up, w_down)
        qT, km, vT, dqT, dkm, dvT, gate = _proj_call(xp, lw, tabs)
        aT = _mla_call(qT, km, vT, B, n_real_tiles)
        lam_init = 0.8 - 0.6 * math.exp(-0.3 * l)
        lam_params = jnp.stack([lambda_q1[l], lambda_k1[l], lambda_q2[l], lambda_k2[l]])
        dT = _diff_call(lam_params, diff_subln[l][:, None], dqT, dkm, tabs["alibi"], dvT,
                        B, n_real_tiles, lam_init)
        last = l == depth - 1
        n_tiles = (n_real_rows // TM) if last else (R // TM)
        xp = _out_call(xp, aT, dT, gate, lw, g_final, n_tiles, last)

    return xp.reshape(B, seq, D)
```

```python
import functools
import math

import jax
import jax.numpy as jnp
from jax import lax
from jax.experimental import pallas as pl
from jax.experimental.pallas import tpu as pltpu

D_MODEL = 1024
N_META = 16
EPS = 1e-6
MLA_HEADS = 8
MLA_Q_RANK = 256
MLA_KV_RANK = 128
MLA_NOPE = 64
MLA_ROPE = 32
MLA_V = 64
ROPE_BASE = 10000.0
DIFF_HEADS = 4
DIFF_HD = 64
DIFF_V = 128
D_FF = 4 * D_MODEL

LOG2E = 1.4426950408889634
NEG = -1e30

CH = 256
TK = 2 * CH
MK = 128
MLA_HEADS_PER_STEP = 4
DIFF_HEADS_PER_STEP = 2
TM = 512
SLOT = 128
MLA_VROWS = 80
DIFF_VROWS = 144
VMEM_LIMIT = 56 * 1024 * 1024

F32 = jnp.float32
BF16 = jnp.bfloat16


def _rms(x, g):
    return x * lax.rsqrt(jnp.mean(x * x, axis=-1, keepdims=True) + EPS) * g


def _dot(a, b):
    return jnp.dot(a, b, preferred_element_type=F32)


def _dot_nt(a, b):
    return lax.dot_general(a, b, (((1,), (1,)), ((), ())), preferred_element_type=F32)


def _dot_tn(a, b):
    return lax.dot_general(a, b, (((0,), (0,)), ((), ())), preferred_element_type=F32)


def _proj_kernel(x_ref, g_ref, w1_ref, gq_ref, gkv_ref, wqT_ref, wk_ref, wvT_ref, onev_ref,
                 cqT_ref, sqT_ref, ck_ref, sk_ref, wdqT_ref, wdk_ref, wdvT_ref, onedv_ref,
                 wg_ref, bg_ref,
                 qT_out, km_out, vT_out, dqT_out, dkm_out, dvT_out, gate_out):
    x = x_ref[...]
    hb = _rms(x, g_ref[...]).astype(BF16)

    p1 = _dot(hb, w1_ref[...])
    qlat = _rms(p1[:, :MLA_Q_RANK], gq_ref[...]).astype(BF16)
    kvlat = _rms(p1[:, MLA_Q_RANK:MLA_Q_RANK + MLA_KV_RANK], gkv_ref[...]).astype(BF16)
    kr = p1[:, 384:512]
    krsw = p1[:, 512:640]

    qT = _dot_nt(wqT_ref[...], qlat)
    cq = cqT_ref[...]
    sq = sqT_ref[...]
    half = MLA_HEADS * SLOT
    for h in range(MLA_HEADS):
        r = (qT[h * SLOT:(h + 1) * SLOT] * cq + qT[half + h * SLOT:half + (h + 1) * SLOT] * sq).astype(BF16)
        for c in range(TM // CH):
            qT_out[c, h * SLOT:(h + 1) * SLOT, :] = r[:, c * CH:(c + 1) * CH]

    krot = kr * ck_ref[...] + krsw * sk_ref[...]
    kn = _dot(kvlat, wk_ref[...])
    for h in range(MLA_HEADS):
        km_out[:, h * SLOT:(h + 1) * SLOT] = (kn[:, h * SLOT:(h + 1) * SLOT] + krot).astype(BF16)

    vT = (_dot_nt(wvT_ref[...], kvlat) + onev_ref[...]).astype(BF16)
    for c in range(TM // CH):
        vT_out[c] = vT[:, c * CH:(c + 1) * CH]

    dqT = (_dot_nt(wdqT_ref[...], hb) * (DIFF_HD ** -0.5 * LOG2E)).astype(BF16)
    for c in range(TM // CH):
        dqT_out[c] = dqT[:, c * CH:(c + 1) * CH]
    dkm_out[...] = _dot(hb, wdk_ref[...]).astype(BF16)
    dvT = (_dot_nt(wdvT_ref[...], hb) + onedv_ref[...]).astype(BF16)
    for c in range(TM // CH):
        dvT_out[c] = dvT[:, c * CH:(c + 1) * CH]

    gate_out[...] = jax.nn.sigmoid(_dot(hb, wg_ref[...]) + bg_ref[...]).astype(BF16)


def _const_spec(shape):
    nd = len(shape)
    return pl.BlockSpec(shape, lambda *_: (0,) * nd)


def _proj_call(xp, lw, tabs):
    R = xp.shape[0]
    nt = R // TM
    nch = R // CH
    per = TM // CH
    row = lambda w: pl.BlockSpec((TM, w), lambda t: (t, 0))
    colT = lambda f: pl.BlockSpec((f, TM), lambda t: (0, t))
    chunk = lambda f: pl.BlockSpec((per, f, CH), lambda t: (t, 0, 0))
    ins = [
        (xp, row(D_MODEL)),
        (lw["g_attn"], _const_spec((1, D_MODEL))),
        (lw["w1"], _const_spec(lw["w1"].shape)),
        (lw["gq"], _const_spec((1, MLA_Q_RANK))),
        (lw["gkv"], _const_spec((1, MLA_KV_RANK))),
        (lw["wqT"], _const_spec(lw["wqT"].shape)),
        (lw["wk"], _const_spec(lw["wk"].shape)),
        (lw["wvT"], _const_spec(lw["wvT"].shape)),
        (tabs["onev"], _const_spec(tabs["onev"].shape)),
        (tabs["cqT"], colT(SLOT)),
        (tabs["sqT"], colT(SLOT)),
        (tabs["ck"], row(SLOT)),
        (tabs["sk"], row(SLOT)),
        (lw["wdqT"], _const_spec(lw["wdqT"].shape)),
        (lw["wdk"], _const_spec(lw["wdk"].shape)),
        (lw["wdvT"], _const_spec(lw["wdvT"].shape)),
        (tabs["onedv"], _const_spec(tabs["onedv"].shape)),
        (lw["wg"], _const_spec(lw["wg"].shape)),
        (lw["bg"], _const_spec((1, 2 * D_MODEL))),
    ]
    out_shape = [
        jax.ShapeDtypeStruct((nch, MLA_HEADS * SLOT, CH), BF16),
        jax.ShapeDtypeStruct((R, MLA_HEADS * SLOT), BF16),
        jax.ShapeDtypeStruct((nch, MLA_HEADS * MLA_VROWS, CH), BF16),
        jax.ShapeDtypeStruct((nch, DIFF_HEADS * SLOT, CH), BF16),
        jax.ShapeDtypeStruct((R, DIFF_HEADS * SLOT), BF16),
        jax.ShapeDtypeStruct((nch, DIFF_HEADS * DIFF_VROWS, CH), BF16),
        jax.ShapeDtypeStruct((R, 2 * D_MODEL), BF16),
    ]
    out_specs = [
        chunk(MLA_HEADS * SLOT),
        row(MLA_HEADS * SLOT),
        chunk(MLA_HEADS * MLA_VROWS),
        chunk(DIFF_HEADS * SLOT),
        row(DIFF_HEADS * SLOT),
        chunk(DIFF_HEADS * DIFF_VROWS),
        row(2 * D_MODEL),
    ]
    return pl.pallas_call(
        _proj_kernel,
        grid=(nt,),
        in_specs=[s for _, s in ins],
        out_specs=out_specs,
        out_shape=out_shape,
        compiler_params=pltpu.CompilerParams(
            dimension_semantics=("arbitrary",), vmem_limit_bytes=VMEM_LIMIT),
        name="proj",
    )(*[a for a, _ in ins])


def _tile_masks(i, n_real_tiles):
    is_meta = i == n_real_tiles
    big = 4 * TK
    krow = lax.broadcasted_iota(jnp.int32, (MK, CH), 0)
    qlane = lax.broadcasted_iota(jnp.int32, (MK, CH), 1)
    meta_mask = (krow < N_META) & (krow <= qlane + jnp.where(is_meta, 0, big))
    krow = lax.broadcasted_iota(jnp.int32, (TK, CH), 0)
    qlane = lax.broadcasted_iota(jnp.int32, (TK, CH), 1)
    last_mask = krow + jnp.where(is_meta, big, -(i % (TK // CH)) * CH) <= qlane
    return meta_mask, last_mask


def _col_max(s):
    while s.shape[0] > 32 and s.shape[0] % 4 == 0:
        r = s.shape[0] // 4
        s = jnp.maximum(jnp.maximum(s[:r], s[r:2 * r]), jnp.maximum(s[2 * r:3 * r], s[3 * r:]))
    return jnp.max(s, axis=0, keepdims=True)


def _softmax_steps(ss, ms, accs, vs, mask=None):
    if mask is not None:
        ss = [jnp.where(mask, s, NEG) for s in ss]
    m_new = [jnp.maximum(m, _col_max(s)) for s, m in zip(ss, ms)]
    ps = [jnp.exp2(s - m).astype(BF16) for s, m in zip(ss, m_new)]
    pvs = [_dot(v, p) for v, p in zip(vs, ps)]
    accs = [acc * jnp.exp2(m - mn) + pv for acc, m, mn, pv in zip(accs, ms, m_new, pvs)]
    return m_new, accs


def _mla_kernel(qT_ref, k_ref, kmeta_ref, vT_ref, vmeta_ref, o_ref, *, n_real_tiles):
    i = pl.program_id(2)
    is_meta = i == n_real_tiles
    j_last = jnp.where(is_meta, 0, i // (TK // CH))
    meta_mask, last_mask = _tile_masks(i, n_real_tiles)
    heads = qT_ref.shape[0] // SLOT

    qs = [qT_ref[h * SLOT:(h + 1) * SLOT, :] for h in range(heads)]

    def k_tile(j, h):
        return k_ref[pl.ds(pl.multiple_of(j * TK, TK), TK), h * SLOT:(h + 1) * SLOT]

    def v_tile(j, h):
        rows = slice(h * MLA_VROWS, (h + 1) * MLA_VROWS)
        return jnp.concatenate([vT_ref[(TK // CH) * j + c, rows, :] for c in range(TK // CH)], axis=1)

    hs = range(heads)
    ss = [_dot(kmeta_ref[:MK, h * SLOT:(h + 1) * SLOT], qs[h]) for h in hs]
    ms = [jnp.full((1, CH), NEG, F32) for _ in hs]
    accs = [jnp.zeros((MLA_VROWS, CH), F32) for _ in hs]
    vs = [vmeta_ref[h * MLA_VROWS:(h + 1) * MLA_VROWS, :MK] for h in hs]
    ms, accs = _softmax_steps(ss, ms, accs, vs, meta_mask)

    def step(j, ms, accs, mask=None):
        ss = [_dot(k_tile(j, h), qs[h]) for h in hs]
        return _softmax_steps(ss, ms, accs, [v_tile(j, h) for h in hs], mask)

    def body(j, carry):
        ms, accs = step(j, carry[:heads], carry[heads:])
        return tuple(ms) + tuple(accs)

    carry = lax.fori_loop(0, j_last, body, tuple(ms) + tuple(accs))
    _, accs = step(j_last, carry[:heads], carry[heads:], last_mask)
    for h in hs:
        o = accs[h][:MLA_V] / accs[h][MLA_V:MLA_V + 1]
        o_ref[h * MLA_V:(h + 1) * MLA_V, :] = o.astype(o_ref.dtype)


def _mla_call(qT, km, vT, B, n_real_tiles):
    hp = MLA_HEADS_PER_STEP
    ngrp = MLA_HEADS // hp
    nch = qT.shape[0]
    R = km.shape[0]
    seq = n_real_tiles * CH
    meta_chunk0 = B * n_real_tiles

    def q_idx(b, g, i):
        return (jnp.where(i == n_real_tiles, meta_chunk0 + b, b * n_real_tiles + i), g, 0)

    in_specs = [
        pl.BlockSpec((None, hp * SLOT, CH), q_idx),
        pl.BlockSpec((seq, hp * SLOT), lambda b, g, i: (b, g)),
        pl.BlockSpec((CH, hp * SLOT), lambda b, g, i: (meta_chunk0 + b, g)),
        pl.BlockSpec((n_real_tiles, hp * MLA_VROWS, CH), lambda b, g, i: (b, g, 0)),
        pl.BlockSpec((None, hp * MLA_VROWS, CH), lambda b, g, i: (meta_chunk0 + b, g, 0)),
    ]
    out_spec = pl.BlockSpec((None, hp * MLA_V, CH), q_idx)
    return pl.pallas_call(
        functools.partial(_mla_kernel, n_real_tiles=n_real_tiles),
        grid=(B, ngrp, n_real_tiles + 1),
        in_specs=in_specs,
        out_specs=out_spec,
        out_shape=jax.ShapeDtypeStruct((nch, MLA_HEADS * MLA_V, CH), BF16),
        compiler_params=pltpu.CompilerParams(
            dimension_semantics=("arbitrary", "arbitrary", "arbitrary"), vmem_limit_bytes=VMEM_LIMIT),
        name="mla_attn",
    )(qT, km, km, vT, vT)


def _diff_kernel(lam_ref, subln_ref, qT_ref, k_ref, kmeta_ref, al_ref, almeta_ref, vT_ref, vmeta_ref,
                 o_ref, *, n_real_tiles, lam_init):
    i = pl.program_id(2)
    is_meta = i == n_real_tiles
    j_last = jnp.where(is_meta, 0, i // (TK // CH))
    meta_mask, last_mask = _tile_masks(i, n_real_tiles)
    heads = qT_ref.shape[0] // SLOT

    rowi = lax.broadcasted_iota(jnp.int32, (SLOT, CH), 0)
    ext = jnp.where(rowi < 3, 1.0, 0.0).astype(BF16)
    qs = []
    for h in range(heads):
        q = qT_ref[h * SLOT:(h + 1) * SLOT, :]
        zero = jnp.zeros_like(q)
        qs.append(jnp.concatenate([jnp.where(rowi < DIFF_HD, q, zero), ext], axis=0))
        qs.append(jnp.concatenate([jnp.where(rowi >= DIFF_HD, q, zero), ext], axis=0))

    def k_tile(j, h):
        rows = pl.ds(pl.multiple_of(j * TK, TK), TK)
        return jnp.concatenate([k_ref[rows, h * SLOT:(h + 1) * SLOT], al_ref[h, rows, :]], axis=1)

    def v_tile(j, h):
        rows = slice(h * DIFF_VROWS, (h + 1) * DIFF_VROWS)
        return jnp.concatenate([vT_ref[(TK // CH) * j + c, rows, :] for c in range(TK // CH)], axis=1)

    nc = 2 * heads
    cs = range(nc)
    kms = [jnp.concatenate([kmeta_ref[:MK, h * SLOT:(h + 1) * SLOT], almeta_ref[h, :MK, :]], axis=1)
           for h in range(heads)]
    ss = [_dot(kms[c // 2], qs[c]) for c in cs]
    ms = [jnp.full((1, CH), NEG, F32) for _ in cs]
    accs = [jnp.zeros((DIFF_VROWS, CH), F32) for _ in cs]
    vs = [vmeta_ref[(c // 2) * DIFF_VROWS:(c // 2 + 1) * DIFF_VROWS, :MK] for c in cs]
    ms, accs = _softmax_steps(ss, ms, accs, vs, meta_mask)

    def step(j, ms, accs, mask=None):
        ks = [k_tile(j, h) for h in range(heads)]
        vt = [v_tile(j, h) for h in range(heads)]
        ss = [_dot(ks[c // 2], qs[c]) for c in cs]
        return _softmax_steps(ss, ms, accs, [vt[c // 2] for c in cs], mask)

    def body(j, carry):
        ms, accs = step(j, carry[:nc], carry[nc:])
        return tuple(ms) + tuple(accs)

    carry = lax.fori_loop(0, j_last, body, tuple(ms) + tuple(accs))
    _, accs = step(j_last, carry[:nc], carry[nc:], last_mask)

    lp = lam_ref[...]
    lam = (jnp.exp(jnp.sum(lp[0:1] * lp[1:2], axis=-1, keepdims=True))
           - jnp.exp(jnp.sum(lp[2:3] * lp[3:4], axis=-1, keepdims=True)) + lam_init)
    for h in range(heads):
        outs = [accs[c][:DIFF_V] / accs[c][DIFF_V:DIFF_V + 1] for c in (2 * h, 2 * h + 1)]
        a = outs[0] - lam * outs[1]
        a = a * lax.rsqrt(jnp.mean(a * a, axis=0, keepdims=True) + EPS) * subln_ref[...] * (1.0 - lam_init)
        o_ref[h * DIFF_V:(h + 1) * DIFF_V, :] = a.astype(o_ref.dtype)


def _diff_call(lam_params, subln_col, dqT, dkm, alibi, dvT, B, n_real_tiles, lam_init):
    nch = dqT.shape[0]
    seq = n_real_tiles * CH
    meta_chunk0 = B * n_real_tiles

    hp = DIFF_HEADS_PER_STEP

    def q_idx(b, g, i):
        return (jnp.where(i == n_real_tiles, meta_chunk0 + b, b * n_real_tiles + i), g, 0)

    in_specs = [
        pl.BlockSpec((4, DIFF_HD), lambda b, g, i: (0, 0)),
        pl.BlockSpec((DIFF_V, 1), lambda b, g, i: (0, 0)),
        pl.BlockSpec((None, hp * SLOT, CH), q_idx),
        pl.BlockSpec((seq, hp * SLOT), lambda b, g, i: (b, g)),
        pl.BlockSpec((CH, hp * SLOT), lambda b, g, i: (meta_chunk0 + b, g)),
        pl.BlockSpec((hp, seq, SLOT), lambda b, g, i: (g, 0, 0)),
        pl.BlockSpec((hp, CH, SLOT), lambda b, g, i: (g, seq // CH, 0)),
        pl.BlockSpec((n_real_tiles, hp * DIFF_VROWS, CH), lambda b, g, i: (b, g, 0)),
        pl.BlockSpec((None, hp * DIFF_VROWS, CH), lambda b, g, i: (meta_chunk0 + b, g, 0)),
    ]
    out_spec = pl.BlockSpec((None, hp * DIFF_V, CH), q_idx)
    return pl.pallas_call(
        functools.partial(_diff_kernel, n_real_tiles=n_real_tiles, lam_init=lam_init),
        grid=(B, DIFF_HEADS // hp, n_real_tiles + 1),
        in_specs=in_specs,
        out_specs=out_spec,
        out_shape=jax.ShapeDtypeStruct((nch, DIFF_HEADS * DIFF_V, CH), BF16),
        compiler_params=pltpu.CompilerParams(
            dimension_semantics=("arbitrary", "arbitrary", "arbitrary"), vmem_limit_bytes=VMEM_LIMIT),
        name="diff_attn",
    )(lam_params, subln_col, dqT, dkm, dkm, alibi, alibi, dvT, dvT)


def _out_kernel(x_ref, aT_ref, dT_ref, gate_ref, wa_ref, wb_ref, wo_ref, gm_ref, wup_ref, wdn_ref,
                gf_ref, o_ref, *, final):
    per = TM // CH
    ya = jnp.concatenate([_dot_tn(aT_ref[c], wa_ref[...]) for c in range(per)], axis=0)
    yb = jnp.concatenate([_dot_tn(dT_ref[c], wb_ref[...]) for c in range(per)], axis=0)
    gate = gate_ref[...].astype(F32)
    y = gate[:, :D_MODEL] * ya + gate[:, D_MODEL:] * yb
    x1 = x_ref[...] + _dot(y.astype(BF16), wo_ref[...])
    h2 = _rms(x1, gm_ref[...]).astype(BF16)
    up = _dot(h2, wup_ref[...])
    act = jnp.square(jnp.maximum(up, 0.0)).astype(BF16)
    x2 = x1 + _dot(act, wdn_ref[...])
    if final:
        x2 = _rms(x2, gf_ref[...])
    o_ref[...] = x2


def _out_call(xp, aT, dT, gate, lw, g_final, n_tiles, final):
    per = TM // CH
    row = lambda w: pl.BlockSpec((TM, w), lambda t: (t, 0))
    chunk = lambda f: pl.BlockSpec((per, f, CH), lambda t: (t, 0, 0))
    in_specs = [
        row(D_MODEL), chunk(MLA_HEADS * MLA_V), chunk(DIFF_HEADS * DIFF_V), row(2 * D_MODEL),
        _const_spec(lw["wa"].shape), _const_spec(lw["wb"].shape), _const_spec(lw["wo"].shape),
        _const_spec((1, D_MODEL)), _const_spec(lw["wup"].shape), _const_spec(lw["wdn"].shape),
        _const_spec((1, D_MODEL)),
    ]
    return pl.pallas_call(
        functools.partial(_out_kernel, final=final),
        grid=(n_tiles,),
        in_specs=in_specs,
        out_specs=row(D_MODEL),
        out_shape=jax.ShapeDtypeStruct((n_tiles * TM, D_MODEL), F32),
        compiler_params=pltpu.CompilerParams(
            dimension_semantics=("arbitrary",), vmem_limit_bytes=VMEM_LIMIT),
        name="out_mlp",
    )(xp, aT, dT, gate, lw["wa"], lw["wb"], lw["wo"], lw["g_mlp"], lw["wup"], lw["wdn"], g_final)


def _top16(v):
    bits = lax.bitcast_convert_type(v, jnp.uint32) & jnp.uint32(0xFFFF0000)
    return lax.bitcast_convert_type(bits, F32)


def _tables(B, seq):
    pos_real = N_META + jnp.arange(seq, dtype=F32)
    pos_meta = jnp.arange(CH, dtype=F32)
    pos = jnp.concatenate([pos_real] * B + [pos_meta] * B)
    inv = 1.0 / (ROPE_BASE ** (jnp.arange(0, MLA_ROPE, 2, dtype=F32) / MLA_ROPE))
    ang = pos[:, None] * inv[None, :]
    cos, sin = jnp.cos(ang), jnp.sin(ang)
    R = pos.shape[0]
    ones = jnp.ones((R, MLA_NOPE), F32)
    zeros_n = jnp.zeros((R, MLA_NOPE), F32)
    pad = jnp.zeros((R, SLOT - MLA_NOPE - MLA_ROPE), F32)
    ck = jnp.concatenate([ones, cos, cos, pad], axis=1)
    sk = jnp.concatenate([zeros_n, -sin, sin, pad], axis=1)
    cq = (MLA_NOPE + MLA_ROPE) ** -0.5 * LOG2E
    tabs = {"ck": ck, "sk": sk, "cqT": (ck * cq).T, "sqT": (sk * cq).T}

    slopes = 2.0 ** (-8.0 * jnp.arange(1, DIFF_HEADS + 1, dtype=F32) / DIFF_HEADS)
    pos1 = jnp.concatenate([pos_real, pos_meta])
    c = (slopes[:, None] * LOG2E) * pos1[None, :]
    c_hi = _top16(c)
    c_mid = _top16(c - c_hi)
    c_lo = c - c_hi - c_mid
    al = jnp.stack([c_hi, c_mid, c_lo], axis=-1).astype(BF16)
    tabs["alibi"] = jnp.pad(al, ((0, 0), (0, 0), (0, SLOT - 3)))

    onev = (jnp.arange(MLA_HEADS * MLA_VROWS) % MLA_VROWS == MLA_V).astype(F32)[:, None]
    onedv = (jnp.arange(DIFF_HEADS * DIFF_VROWS) % DIFF_VROWS == DIFF_V).astype(F32)[:, None]
    tabs["onev"] = onev
    tabs["onedv"] = onedv
    return tabs


def _layer_weights(l, attn_norm, w_in, b_gate, mla_q_norm, w_q_up, mla_kv_norm, w_kv_up,
                   w_a_proj, w_b_proj, w_o, mlp_norm, w_up, w_down):
    wi = w_in[l]
    o = 0
    parts = []
    for n in (MLA_Q_RANK, MLA_KV_RANK, MLA_ROPE, 512, 512, 512, D_MODEL, D_MODEL):
        parts.append(wi[:, o:o + n])
        o += n
    w_qa, w_kva, w_kr, w_dq, w_dk, w_dv, w_ga, w_gb = parts
    zl = jnp.zeros((D_MODEL, MLA_NOPE), F32)
    zr = jnp.zeros((D_MODEL, SLOT - MLA_NOPE - MLA_ROPE), F32)
    hr = MLA_ROPE // 2
    kr_slot = jnp.concatenate([zl, w_kr, zr], axis=1)
    krsw_slot = jnp.concatenate([zl, w_kr[:, hr:], w_kr[:, :hr], zr], axis=1)
    w1 = jnp.concatenate([w_qa, w_kva, kr_slot, krsw_slot], axis=1).astype(BF16)

    wq = w_q_up[l].reshape(MLA_Q_RANK, MLA_HEADS, MLA_NOPE + MLA_ROPE)
    qn, qr = wq[..., :MLA_NOPE], wq[..., MLA_NOPE:]
    z32 = jnp.zeros((MLA_Q_RANK, MLA_HEADS, SLOT - MLA_NOPE - MLA_ROPE), F32)
    z64 = jnp.zeros((MLA_Q_RANK, MLA_HEADS, MLA_NOPE), F32)
    raw = jnp.concatenate([qn, qr, z32], axis=-1).reshape(MLA_Q_RANK, MLA_HEADS * SLOT)
    sw = jnp.concatenate([z64, qr[..., hr:], qr[..., :hr], z32], axis=-1).reshape(MLA_Q_RANK, MLA_HEADS * SLOT)
    wqT = jnp.concatenate([raw, sw], axis=1).T.astype(BF16)

    wkv = w_kv_up[l].reshape(MLA_KV_RANK, MLA_HEADS, MLA_NOPE + MLA_V)
    kn, vv = wkv[..., :MLA_NOPE], wkv[..., MLA_NOPE:]
    wk = jnp.concatenate([kn, jnp.zeros_like(kn)], axis=-1).reshape(MLA_KV_RANK, MLA_HEADS * SLOT).astype(BF16)
    wvT = jnp.pad(vv, ((0, 0), (0, 0), (0, MLA_VROWS - MLA_V))).reshape(
        MLA_KV_RANK, MLA_HEADS * MLA_VROWS).T.astype(BF16)

    wdvT = jnp.pad(w_dv.reshape(D_MODEL, DIFF_HEADS, DIFF_V), ((0, 0), (0, 0), (0, DIFF_VROWS - DIFF_V))
                   ).reshape(D_MODEL, DIFF_HEADS * DIFF_VROWS).T.astype(BF16)
    return {
        "g_attn": attn_norm[l][None, :], "w1": w1,
        "gq": mla_q_norm[l][None, :], "gkv": mla_kv_norm[l][None, :],
        "wqT": wqT, "wk": wk, "wvT": wvT,
        "wdqT": w_dq.T.astype(BF16), "wdk": w_dk.astype(BF16), "wdvT": wdvT,
        "wg": jnp.concatenate([w_ga, w_gb], axis=1).astype(BF16),
        "bg": jnp.concatenate([b_gate[l, 0], b_gate[l, 1]])[None, :],
        "wa": w_a_proj[l].astype(BF16), "wb": w_b_proj[l].astype(BF16), "wo": w_o[l].astype(BF16),
        "g_mlp": mlp_norm[l][None, :], "wup": w_up[l].astype(BF16), "wdn": w_down[l].astype(BF16),
    }


def kernel(x, meta_tokens, attn_norm, w_in, b_gate, mla_q_norm, w_q_up, mla_kv_norm, w_kv_up, lambda_q1, lambda_k1, lambda_q2, lambda_k2, diff_subln, w_a_proj, w_b_proj, w_o, mlp_norm, w_up, w_down, final_norm):
    B, seq, D = x.shape
    depth = attn_norm.shape[0]
    assert D == D_MODEL and seq % TM == 0 and (B * CH) % TM == 0 and meta_tokens.shape[0] == N_META
    n_real_tiles = seq // CH
    n_real_rows = B * seq

    meta_blk = jnp.pad(meta_tokens.astype(x.dtype), ((0, CH - N_META), (0, 0)))
    xp = jnp.concatenate([x.reshape(n_real_rows, D)] + [meta_blk] * B, axis=0)
    R = xp.shape[0]
    tabs = _tables(B, seq)
    g_final = final_norm[None, :]

    for l in range(depth):
        lw = _layer_weights(l, attn_norm, w_in, b_gate, mla_q_norm, w_q_up, mla_kv_norm, w_kv_up,
                            w_a_proj, w_b_proj, w_o, mlp_norm, w_up, w_down)
        qT, km, vT, dqT, dkm, dvT, gate = _proj_call(xp, lw, tabs)
        aT = _mla_call(qT, km, vT, B, n_real_tiles)
        lam_init = 0.8 - 0.6 * math.exp(-0.3 * l)
        lam_params = jnp.stack([lambda_q1[l], lambda_k1[l], lambda_q2[l], lambda_k2[l]])
        dT = _diff_call(lam_params, diff_subln[l][:, None], dqT, dkm, tabs["alibi"], dvT,
                        B, n_real_tiles, lam_init)
        last = l == depth - 1
        n_tiles = (n_real_rows // TM) if last else (R // TM)
        xp = _out_call(xp, aT, dT, gate, lw, g_final, n_tiles, last)

    return xp.reshape(B, seq, D)
```

```python
import functools
import math

import jax
import jax.numpy as jnp
from jax import lax
from jax.experimental import pallas as pl
from jax.experimental.pallas import tpu as pltpu

D_MODEL = 1024
N_META = 16
EPS = 1e-6
MLA_HEADS = 8
MLA_Q_RANK = 256
MLA_KV_RANK = 128
MLA_NOPE = 64
MLA_ROPE = 32
MLA_V = 64
ROPE_BASE = 10000.0
DIFF_HEADS = 4
DIFF_HD = 64
DIFF_V = 128
D_FF = 4 * D_MODEL

LOG2E = 1.4426950408889634
NEG = -1e30

CH = 256
TK = 2 * CH
MK = N_META
KEY_TILES_PER_BLOCK = 2
SUM_LIMIT = 2.0 ** 100
SUM_FLOOR = 2.0 ** -64
SKIP_MARGIN = 125.0
NORM_SLACK = 1.05
DECAY_HEADS = 2
MLA_HEADS_PER_STEP = 8
DIFF_HEADS_PER_STEP = 4
TM = 512
SLOT = 128
V_EXTRA_ROWS = 16
MLA_VROWS = MLA_V + V_EXTRA_ROWS
DIFF_VROWS = DIFF_V + V_EXTRA_ROWS
VMEM_LIMIT = 56 * 1024 * 1024

F32 = jnp.float32
BF16 = jnp.bfloat16


def _rms(x, g):
    return x * lax.rsqrt(jnp.mean(x * x, axis=-1, keepdims=True) + EPS) * g


def _dot(a, b):
    return jnp.dot(a, b, preferred_element_type=F32)


def _dot_nt(a, b):
    return lax.dot_general(a, b, (((1,), (1,)), ((), ())), preferred_element_type=F32)


def _dot_tn(a, b):
    return lax.dot_general(a, b, (((0,), (0,)), ((), ())), preferred_element_type=F32)


def _pick_rows(xa_ref, xb_ref, tail):
    if tail is None:
        return xa_ref[...]
    return jnp.where(pl.program_id(0) == tail, xb_ref[...], xa_ref[...])


def _row_sources(xa, xb):
    na = xa.shape[0] // TM
    ib = xb.shape[0] // TM - 1
    return [pl.BlockSpec((TM, D_MODEL), lambda t: (jnp.minimum(t, na - 1), 0)),
            pl.BlockSpec((TM, D_MODEL), lambda t: (ib, 0))]


def _layer_spec(arr, l):
    nd = arr.ndim
    return pl.BlockSpec((None,) + arr.shape[1:], lambda *_: (l,) + (0,) * (nd - 1))


def _proj_kernel(xa_ref, xb_ref, g_ref, w1_ref, gq_ref, gkv_ref, wqT_ref, wk_ref, wvT_ref, onev_ref,
                 cqT_ref, sqT_ref, ck_ref, sk_ref, wdqT_ref, wdk_ref, wdvT_ref, onedv_ref,
                 wg_ref, bg_ref, grp_ref,
                 qT_out, km_out, vT_out, dqT_out, dkm_out, dvT_out, gate_out, kn_out, *, tail):
    x = _pick_rows(xa_ref, xb_ref, tail)
    hb = _rms(x, g_ref[...]).astype(BF16)

    p1 = _dot(hb, w1_ref[...])
    qlat = _rms(p1[:, :MLA_Q_RANK], gq_ref[...]).astype(BF16)
    kvlat = _rms(p1[:, MLA_Q_RANK:MLA_Q_RANK + MLA_KV_RANK], gkv_ref[...]).astype(BF16)
    kr = p1[:, 384:512]
    krsw = p1[:, 512:640]

    qT = _dot_nt(wqT_ref[...], qlat)
    cq = cqT_ref[...]
    sq = sqT_ref[...]
    half = MLA_HEADS * SLOT
    for h in range(MLA_HEADS):
        r = (qT[h * SLOT:(h + 1) * SLOT] * cq + qT[half + h * SLOT:half + (h + 1) * SLOT] * sq).astype(BF16)
        for c in range(TM // CH):
            qT_out[c, h * SLOT:(h + 1) * SLOT, :] = r[:, c * CH:(c + 1) * CH]

    krot = kr * ck_ref[...] + krsw * sk_ref[...]
    kn = _dot(kvlat, wk_ref[...])
    for h in range(MLA_HEADS):
        km_out[:, h * SLOT:(h + 1) * SLOT] = (kn[:, h * SLOT:(h + 1) * SLOT] + krot).astype(BF16)

    vT = (_dot_nt(wvT_ref[...], kvlat) + onev_ref[...]).astype(BF16)
    for c in range(TM // CH):
        vT_out[c] = vT[:, c * CH:(c + 1) * CH]

    dqT = (_dot_nt(wdqT_ref[...], hb) * (DIFF_HD ** -0.5 * LOG2E)).astype(BF16)
    for c in range(TM // CH):
        dqT_out[c] = dqT[:, c * CH:(c + 1) * CH]
    dk = _dot(hb, wdk_ref[...]).astype(BF16)
    dkm_out[...] = dk
    dkf = dk.astype(F32)
    n2 = _dot((dkf * dkf).astype(BF16), grp_ref[...])
    kn_out[...] = jnp.broadcast_to(jnp.max(n2, axis=0, keepdims=True), kn_out.shape)
    dvT = (_dot_nt(wdvT_ref[...], hb) + onedv_ref[...]).astype(BF16)
    for c in range(TM // CH):
        dvT_out[c] = dvT[:, c * CH:(c + 1) * CH]

    gate_out[...] = jax.nn.sigmoid(_dot(hb, wg_ref[...]) + bg_ref[...]).astype(BF16)


def _const_spec(shape):
    nd = len(shape)
    return pl.BlockSpec(shape, lambda *_: (0,) * nd)


def _proj_call(xa, xb, R, l, W, tabs):
    nt = R // TM
    nch = R // CH
    per = TM // CH
    n_real = nt - 1
    tiles_per_batch = tabs["ck"].shape[0] // TM - 1

    def tab_idx(t):
        return jnp.where(t < n_real, t % tiles_per_batch, tiles_per_batch)

    row = lambda w: pl.BlockSpec((TM, w), lambda t: (t, 0))
    row_tab = pl.BlockSpec((TM, SLOT), lambda t: (tab_idx(t), 0))
    col_tab = pl.BlockSpec((SLOT, TM), lambda t: (0, tab_idx(t)))
    chunk = lambda f: pl.BlockSpec((per, f, CH), lambda t: (t, 0, 0))
    lay = lambda name: (W[name], _layer_spec(W[name], l))
    xa_spec, xb_spec = _row_sources(xa, xb)
    ins = [
        (xa, xa_spec), (xb, xb_spec),
        lay("g_attn"), lay("w1"), lay("gq"), lay("gkv"), lay("wqT"), lay("wk"), lay("wvT"),
        (tabs["onev"], _const_spec(tabs["onev"].shape)),
        (tabs["cqT"], col_tab), (tabs["sqT"], col_tab), (tabs["ck"], row_tab), (tabs["sk"], row_tab),
        lay("wdqT"), lay("wdk"), lay("wdvT"),
        (tabs["onedv"], _const_spec(tabs["onedv"].shape)),
        lay("wg"), lay("bg"),
        (tabs["groups"], _const_spec(tabs["groups"].shape)),
    ]
    out_shape = [
        jax.ShapeDtypeStruct((nch, MLA_HEADS * SLOT, CH), BF16),
        jax.ShapeDtypeStruct((R, MLA_HEADS * SLOT), BF16),
        jax.ShapeDtypeStruct((nch, MLA_HEADS * MLA_VROWS, CH), BF16),
        jax.ShapeDtypeStruct((nch, DIFF_HEADS * SLOT, CH), BF16),
        jax.ShapeDtypeStruct((R, DIFF_HEADS * SLOT), BF16),
        jax.ShapeDtypeStruct((nch, DIFF_HEADS * DIFF_VROWS, CH), BF16),
        jax.ShapeDtypeStruct((R, 2 * D_MODEL), BF16),
        jax.ShapeDtypeStruct((nt, 8, SLOT), F32),
    ]
    out_specs = [
        chunk(MLA_HEADS * SLOT),
        row(MLA_HEADS * SLOT),
        chunk(MLA_HEADS * MLA_VROWS),
        chunk(DIFF_HEADS * SLOT),
        row(DIFF_HEADS * SLOT),
        chunk(DIFF_HEADS * DIFF_VROWS),
        row(2 * D_MODEL),
        pl.BlockSpec((None, 8, SLOT), lambda t: (t, 0, 0)),
    ]
    return pl.pallas_call(
        functools.partial(_proj_kernel, tail=None if xa.shape[0] == R else n_real),
        grid=(nt,),
        in_specs=[s for _, s in ins],
        out_specs=out_specs,
        out_shape=out_shape,
        compiler_params=pltpu.CompilerParams(
            dimension_semantics=("arbitrary",), vmem_limit_bytes=VMEM_LIMIT),
        name="proj",
    )(*[a for a, _ in ins])


def _tile_masks(i, n_real_tiles):
    is_meta = i == n_real_tiles
    big = 4 * TK
    krow = lax.broadcasted_iota(jnp.int32, (MK, CH), 0)
    qlane = lax.broadcasted_iota(jnp.int32, (MK, CH), 1)
    meta_mask = (krow < N_META) & (krow <= qlane + jnp.where(is_meta, 0, big))
    krow = lax.broadcasted_iota(jnp.int32, (TK, CH), 0)
    qlane = lax.broadcasted_iota(jnp.int32, (TK, CH), 1)
    last_mask = krow + jnp.where(is_meta, big, -(i % (TK // CH)) * CH) <= qlane
    return meta_mask, last_mask


def _col_max(s):
    while s.shape[0] > 32 and s.shape[0] % 4 == 0:
        r = s.shape[0] // 4
        s = jnp.maximum(jnp.maximum(s[:r], s[r:2 * r]), jnp.maximum(s[2 * r:3 * r], s[3 * r:]))
    return jnp.max(s, axis=0, keepdims=True)


def _softmax_steps(ss, ms, accs, vs, mask=None):
    if mask is not None:
        ss = [jnp.where(mask, s, NEG) for s in ss]
    m_new = [jnp.maximum(m, _col_max(s)) for s, m in zip(ss, ms)]
    ps = [jnp.exp2(s - m).astype(BF16) for s, m in zip(ss, m_new)]
    pvs = [_dot(v, p) for v, p in zip(vs, ps)]
    accs = [acc * jnp.exp2(m - mn) + pv for acc, m, mn, pv in zip(accs, ms, m_new, pvs)]
    return m_new, accs


def _attn_scratch(nc, vrows):
    return [pltpu.VMEM((nc, 1, CH), F32),
            pltpu.VMEM((nc, vrows, CH), F32)]


def _attn_core(i, n_real_tiles, qs, slopes, kmetas, vmetas, k_tiles, v_tiles, scr, finalize, decay=None):
    m_scr, acc_scr = scr
    cs = range(len(qs))
    kb = KEY_TILES_PER_BLOCK
    lrow = acc_scr.shape[1] - V_EXTRA_ROWS
    is_meta = i == n_real_tiles
    n = jnp.where(is_meta, 0, i // (TK // CH))
    n_groups = n // kb
    meta_mask, last_mask = _tile_masks(i, n_real_tiles)
    lane = lax.broadcasted_iota(jnp.int32, (1, CH), 1)
    ahead = ((i * CH + 1 + lane) * jnp.where(is_meta, 0, 1)).astype(F32)

    def run(tiles, lead, act=None, fresh=False, bias_after_lead=False):
        act = list(cs) if act is None else act
        if fresh:
            ms = [jnp.full((1, CH), NEG, F32) for _ in act]
            accs = [jnp.zeros(acc_scr.shape[1:], F32) for _ in act]
        else:
            ms = [m_scr[c] for c in act]
            accs = [acc_scr[c] for c in act]
        sss = [[_dot(ks[c], qs[c]) for c in act] for ks, _, _ in tiles]
        for idx, (ss, (_, vs, mask)) in enumerate(zip(sss, tiles)):
            vs = [vs[c] for c in act]
            if idx < lead:
                ms, accs = _softmax_steps(ss, ms, accs, vs, mask)
                if bias_after_lead and idx == lead - 1:
                    for j, c in enumerate(act):
                        if slopes[c]:
                            off = slopes[c] * ahead
                            accs[j] = accs[j] * jnp.exp2(-off)
                            ms[j] = ms[j] + off
            else:
                if mask is not None:
                    ss = [jnp.where(mask, s, NEG) for s in ss]
                ps = [jnp.exp2(s - m).astype(BF16) for s, m in zip(ss, ms)]
                pvs = [_dot(v, p) for v, p in zip(vs, ps)]
                accs = [acc + pv for acc, pv in zip(accs, pvs)]
        for j, c in enumerate(act):
            m_scr[c] = ms[j]
            acc_scr[c] = accs[j]

    def full(t):
        return (k_tiles(t), v_tiles(t), None)

    def first_tiles(r, odd):
        diag = (k_tiles(n), v_tiles(n), last_mask) if odd else (k_tiles(n, CH), v_tiles(n, CH), last_mask[:CH])
        return [(kmetas(), vmetas(), meta_mask), diag] + [full(n - 1 - u) for u in range(r)]

    def group_tiles(g):
        base = (n_groups - 1 - g) * kb
        return [full(base + kb - 1 - u) for u in range(kb)]

    def far_limit(chains):
        knorm_ref = decay[0]
        a = (i * CH + 1 + lane * 0).astype(F32)
        need = None
        for c in chains:
            qn = jnp.sqrt(jnp.sum(jnp.square(qs[c][:SLOT].astype(F32)), axis=0, keepdims=True))
            mm = m_scr[c] - slopes[c] * ahead
            nc = (qn * NORM_SLACK * knorm_ref[c:c + 1, :] - mm + SKIP_MARGIN) * (1.0 / slopes[c]) + N_META
            need = nc if need is None else jnp.maximum(need, nc)
        g = jnp.max(jnp.floor((need - a) * (1.0 / (kb * TK))), axis=1, keepdims=True)
        return jnp.clip(g.astype(jnp.int32)[0, 0] + n_groups + 1, 0, n_groups)

    def sweep(fixed):
        for r in range(kb):
            for odd in range(TK // CH):
                @pl.when((n % kb == r) & (i % (TK // CH) == odd))
                def _(r=r, odd=odd):
                    run(first_tiles(r, odd), 1 if fixed else 2 + r, fresh=True, bias_after_lead=fixed)

        def group(g, carry):
            run(group_tiles(g), 0 if fixed else kb)
            return carry

        if fixed and decay is not None:
            limits = [far_limit(chains) for chains in decay[1]]
            act, start = list(cs), 0
            for chains, limit in zip(decay[1], limits):
                stop = jnp.maximum(limit, start)

                def part(g, carry, act=tuple(act)):
                    run(group_tiles(g), 0, act=list(act))
                    return carry

                lax.fori_loop(start, stop, part, 0)
                act, start = [c for c in act if c not in chains], stop

            def rest(g, carry, act=tuple(act)):
                run(group_tiles(g), 0, act=list(act))
                return carry

            lax.fori_loop(start, n_groups, rest, 0)
        else:
            lax.fori_loop(0, n_groups, group, 0)

    sweep(fixed=True)
    sums = [acc_scr[c][lrow:lrow + 1] for c in cs]
    unsafe = [jnp.where((s > SUM_FLOOR) & (s < SUM_LIMIT), 0.0, 1.0) for s in sums]

    @pl.when(jnp.max(functools.reduce(jnp.maximum, unsafe)) > 0.5)
    def _():
        sweep(fixed=False)

    finalize([acc_scr[c] for c in cs])


def _mla_kernel(qT_ref, k_ref, kmeta_ref, vT_ref, vmeta_ref, o_ref, *scr, n_real_tiles):
    i = pl.program_id(2)
    heads = qT_ref.shape[0] // SLOT
    hs = range(heads)
    qs = [qT_ref[h * SLOT:(h + 1) * SLOT, :] for h in hs]

    def k_tiles(t, keys=TK):
        rows = pl.ds(pl.multiple_of(t * TK, TK), keys)
        return [k_ref[rows, h * SLOT:(h + 1) * SLOT] for h in hs]

    def v_tiles(t, keys=TK):
        return [jnp.concatenate([vT_ref[(TK // CH) * t + c, h * MLA_VROWS:(h + 1) * MLA_VROWS, :]
                                 for c in range(keys // CH)], axis=1) for h in hs]

    def kmetas():
        return [kmeta_ref[:MK, h * SLOT:(h + 1) * SLOT] for h in hs]

    def vmetas():
        return [vmeta_ref[h * MLA_VROWS:(h + 1) * MLA_VROWS, :MK] for h in hs]

    def finalize(accs):
        for h in hs:
            o = accs[h][:MLA_V] / accs[h][MLA_V:MLA_V + 1]
            o_ref[h * MLA_V:(h + 1) * MLA_V, :] = o.astype(o_ref.dtype)

    _attn_core(i, n_real_tiles, qs, [0.0] * heads, kmetas, vmetas, k_tiles, v_tiles, scr, finalize)


def _mla_call(qT, km, vT, B, n_real_tiles):
    hp = MLA_HEADS_PER_STEP
    ngrp = MLA_HEADS // hp
    nch = qT.shape[0]
    R = km.shape[0]
    seq = n_real_tiles * CH
    meta_chunk0 = B * n_real_tiles

    def q_idx(b, g, i):
        return (jnp.where(i == n_real_tiles, meta_chunk0 + b, b * n_real_tiles + i), g, 0)

    in_specs = [
        pl.BlockSpec((None, hp * SLOT, CH), q_idx),
        pl.BlockSpec((seq, hp * SLOT), lambda b, g, i: (b, g), pipeline_mode=pl.Buffered(1)),
        pl.BlockSpec((CH, hp * SLOT), lambda b, g, i: (meta_chunk0 + b, g)),
        pl.BlockSpec((n_real_tiles, hp * MLA_VROWS, CH), lambda b, g, i: (b, g, 0), pipeline_mode=pl.Buffered(1)),
        pl.BlockSpec((None, hp * MLA_VROWS, CH), lambda b, g, i: (meta_chunk0 + b, g, 0)),
    ]
    out_spec = pl.BlockSpec((None, hp * MLA_V, CH), q_idx)
    return pl.pallas_call(
        functools.partial(_mla_kernel, n_real_tiles=n_real_tiles),
        grid=(B, ngrp, n_real_tiles + 1),
        in_specs=in_specs,
        out_specs=out_spec,
        out_shape=jax.ShapeDtypeStruct((nch, MLA_HEADS * MLA_V, CH), BF16),
        scratch_shapes=_attn_scratch(hp, MLA_VROWS),
        compiler_params=pltpu.CompilerParams(
            dimension_semantics=("arbitrary", "arbitrary", "arbitrary"), vmem_limit_bytes=VMEM_LIMIT),
        name="mla_attn",
    )(qT, km, km, vT, vT)


def _diff_kernel(lam_ref, subln_ref, knorm_ref, qT_ref, k_ref, kmeta_ref, al_ref, almeta_ref, vT_ref, vmeta_ref,
                 o_ref, *scr, n_real_tiles, lam_init):
    i = pl.program_id(2)
    heads = qT_ref.shape[0] // SLOT

    rowi = lax.broadcasted_iota(jnp.int32, (SLOT, CH), 0)
    ext = jnp.where(rowi < 3, 1.0, 0.0).astype(BF16)
    qs = []
    for h in range(heads):
        q = qT_ref[h * SLOT:(h + 1) * SLOT, :]
        zero = jnp.zeros_like(q)
        qs.append(jnp.concatenate([jnp.where(rowi < DIFF_HD, q, zero), ext], axis=0))
        qs.append(jnp.concatenate([jnp.where(rowi >= DIFF_HD, q, zero), ext], axis=0))

    def per_map(xs):
        return [x for x in xs for _ in range(2)]

    def k_tiles(t, keys=TK):
        rows = pl.ds(pl.multiple_of(t * TK, TK), keys)
        return per_map([jnp.concatenate([k_ref[rows, h * SLOT:(h + 1) * SLOT], al_ref[h, rows, :]], axis=1)
                        for h in range(heads)])

    def v_tiles(t, keys=TK):
        return per_map([jnp.concatenate([vT_ref[(TK // CH) * t + c, h * DIFF_VROWS:(h + 1) * DIFF_VROWS, :]
                                         for c in range(keys // CH)], axis=1) for h in range(heads)])

    def kmetas():
        return per_map([jnp.concatenate([kmeta_ref[:MK, h * SLOT:(h + 1) * SLOT], almeta_ref[h, :MK, :]], axis=1)
                        for h in range(heads)])

    def vmetas():
        return per_map([vmeta_ref[h * DIFF_VROWS:(h + 1) * DIFF_VROWS, :MK] for h in range(heads)])

    def finalize(accs):
        lp = lam_ref[...]
        lam = (jnp.exp(jnp.sum(lp[0:1] * lp[1:2], axis=-1, keepdims=True))
               - jnp.exp(jnp.sum(lp[2:3] * lp[3:4], axis=-1, keepdims=True)) + lam_init)
        for h in range(heads):
            outs = [accs[c][:DIFF_V] / accs[c][DIFF_V:DIFF_V + 1] for c in (2 * h, 2 * h + 1)]
            a = outs[0] - lam * outs[1]
            a = a * lax.rsqrt(jnp.mean(a * a, axis=0, keepdims=True) + EPS) * subln_ref[...] * (1.0 - lam_init)
            o_ref[h * DIFF_V:(h + 1) * DIFF_V, :] = a.astype(o_ref.dtype)

    assert heads == DIFF_HEADS
    slopes = per_map([2.0 ** (-8.0 * (h + 1) / DIFF_HEADS) * LOG2E for h in range(heads)])
    _attn_core(i, n_real_tiles, qs, slopes, kmetas, vmetas, k_tiles, v_tiles, scr, finalize,
               decay=(knorm_ref, [[2 * h, 2 * h + 1] for h in range(DECAY_HEADS)]))


def _diff_call(l, lam_params, subln_col, knorm, dqT, dkm, alibi, dvT, B, n_real_tiles, lam_init):
    nch = dqT.shape[0]
    seq = n_real_tiles * CH
    meta_chunk0 = B * n_real_tiles

    hp = DIFF_HEADS_PER_STEP

    def q_idx(b, g, i):
        return (jnp.where(i == n_real_tiles, meta_chunk0 + b, b * n_real_tiles + i), g, 0)

    in_specs = [
        _layer_spec(lam_params, l),
        _layer_spec(subln_col, l),
        pl.BlockSpec((None,) + knorm.shape[1:], lambda b, g, i: (b, 0, 0)),
        pl.BlockSpec((None, hp * SLOT, CH), q_idx),
        pl.BlockSpec((seq, hp * SLOT), lambda b, g, i: (b, g), pipeline_mode=pl.Buffered(1)),
        pl.BlockSpec((CH, hp * SLOT), lambda b, g, i: (meta_chunk0 + b, g)),
        pl.BlockSpec((hp, seq, SLOT), lambda b, g, i: (g, 0, 0), pipeline_mode=pl.Buffered(1)),
        pl.BlockSpec((hp, CH, SLOT), lambda b, g, i: (g, seq // CH, 0)),
        pl.BlockSpec((n_real_tiles, hp * DIFF_VROWS, CH), lambda b, g, i: (b, g, 0), pipeline_mode=pl.Buffered(1)),
        pl.BlockSpec((None, hp * DIFF_VROWS, CH), lambda b, g, i: (meta_chunk0 + b, g, 0)),
    ]
    out_spec = pl.BlockSpec((None, hp * DIFF_V, CH), q_idx)
    return pl.pallas_call(
        functools.partial(_diff_kernel, n_real_tiles=n_real_tiles, lam_init=lam_init),
        grid=(B, DIFF_HEADS // hp, n_real_tiles + 1),
        in_specs=in_specs,
        out_specs=out_spec,
        out_shape=jax.ShapeDtypeStruct((nch, DIFF_HEADS * DIFF_V, CH), BF16),
        scratch_shapes=_attn_scratch(2 * hp, DIFF_VROWS),
        compiler_params=pltpu.CompilerParams(
            dimension_semantics=("arbitrary", "arbitrary", "arbitrary"), vmem_limit_bytes=VMEM_LIMIT),
        name="diff_attn",
    )(lam_params, subln_col, knorm, dqT, dkm, dkm, alibi, alibi, dvT, dvT)


def _out_kernel(xa_ref, xb_ref, aT_ref, dT_ref, gate_ref, wa_ref, wb_ref, wo_ref, gm_ref, wup_ref, wdn_ref,
                gf_ref, o_ref, *, final, tail):
    per = TM // CH
    ya = jnp.concatenate([_dot_tn(aT_ref[c], wa_ref[...]) for c in range(per)], axis=0)
    yb = jnp.concatenate([_dot_tn(dT_ref[c], wb_ref[...]) for c in range(per)], axis=0)
    gate = gate_ref[...].astype(F32)
    y = gate[:, :D_MODEL] * ya + gate[:, D_MODEL:] * yb
    x1 = _pick_rows(xa_ref, xb_ref, tail) + _dot(y.astype(BF16), wo_ref[...])
    h2 = _rms(x1, gm_ref[...]).astype(BF16)
    up = _dot(h2, wup_ref[...])
    act = jnp.square(jnp.maximum(up, 0.0)).astype(BF16)
    x2 = x1 + _dot(act, wdn_ref[...])
    if final:
        x2 = _rms(x2, gf_ref[...])
    o_ref[...] = x2


def _out_call(xa, xb, R, aT, dT, gate, l, W, g_final, n_tiles, final):
    per = TM // CH
    row = lambda w: pl.BlockSpec((TM, w), lambda t: (t, 0))
    chunk = lambda f: pl.BlockSpec((per, f, CH), lambda t: (t, 0, 0))
    names = ["wa", "wb", "wo", "g_mlp", "wup", "wdn"]
    in_specs = (_row_sources(xa, xb)
                + [chunk(MLA_HEADS * MLA_V), chunk(DIFF_HEADS * DIFF_V), row(2 * D_MODEL)]
                + [_layer_spec(W[k], l) for k in names]
                + [_const_spec((1, D_MODEL))])
    return pl.pallas_call(
        functools.partial(_out_kernel, final=final, tail=None if xa.shape[0] == R else R // TM - 1),
        grid=(n_tiles,),
        in_specs=in_specs,
        out_specs=row(D_MODEL),
        out_shape=jax.ShapeDtypeStruct((n_tiles * TM, D_MODEL), F32),
        compiler_params=pltpu.CompilerParams(
            dimension_semantics=("arbitrary",), vmem_limit_bytes=VMEM_LIMIT),
        name="out_mlp",
    )(xa, xb, aT, dT, gate, *[W[k] for k in names], g_final)


def _top16(v):
    bits = lax.bitcast_convert_type(v, jnp.uint32) & jnp.uint32(0xFFFF0000)
    return lax.bitcast_convert_type(bits, F32)


def _tables(B, seq):
    pos_real = N_META + jnp.arange(seq, dtype=F32)
    pos_meta = jnp.arange(CH, dtype=F32)
    pos = jnp.concatenate([pos_real] + [pos_meta] * (TM // CH))
    inv = 1.0 / (ROPE_BASE ** (jnp.arange(0, MLA_ROPE, 2, dtype=F32) / MLA_ROPE))
    ang = inv[:, None] * pos[None, :]
    cos, sin = jnp.cos(ang), jnp.sin(ang)
    n = pos.shape[0]
    ones = jnp.ones((MLA_NOPE, n), F32)
    zeros_n = jnp.zeros((MLA_NOPE, n), F32)
    pad = jnp.zeros((SLOT - MLA_NOPE - MLA_ROPE, n), F32)
    ckT = jnp.concatenate([ones, cos, cos, pad], axis=0)
    skT = jnp.concatenate([zeros_n, -sin, sin, pad], axis=0)
    cq = (MLA_NOPE + MLA_ROPE) ** -0.5 * LOG2E
    tabs = {"ck": ckT.T, "sk": skT.T, "cqT": ckT * cq, "sqT": skT * cq}

    slopes = 2.0 ** (-8.0 * jnp.arange(1, DIFF_HEADS + 1, dtype=F32) / DIFF_HEADS)
    pos1 = jnp.concatenate([pos_real, pos_meta])
    c = (slopes[:, None] * LOG2E) * pos1[None, :]
    c_hi = _top16(c)
    c_mid = _top16(c - c_hi)
    c_lo = c - c_hi - c_mid
    al = jnp.stack([c_hi, c_mid, c_lo], axis=-1).astype(BF16)
    tabs["alibi"] = jnp.pad(al, ((0, 0), (0, 0), (0, SLOT - 3)))

    onev = (jnp.arange(MLA_HEADS * MLA_VROWS) % MLA_VROWS == MLA_V).astype(F32)[:, None]
    onedv = (jnp.arange(DIFF_HEADS * DIFF_VROWS) % DIFF_VROWS == DIFF_V).astype(F32)[:, None]
    tabs["onev"] = onev
    tabs["onedv"] = onedv
    tabs["groups"] = (jnp.arange(DIFF_HEADS * SLOT)[:, None] // DIFF_HD == jnp.arange(SLOT)[None, :]).astype(BF16)
    return tabs


def _key_norm_table(kn2, B, tiles_per_batch):
    per_tile = kn2[:, 0, :2 * DIFF_HEADS]
    real = per_tile[:B * tiles_per_batch].reshape(B, tiles_per_batch, -1).max(axis=1)
    top = jnp.sqrt(jnp.maximum(real, per_tile[-1][None, :]))
    return jnp.broadcast_to(top[:, :, None], top.shape + (CH,))


def _prep_weights(attn_norm, w_in, b_gate, mla_q_norm, w_q_up, mla_kv_norm, w_kv_up,
                  w_a_proj, w_b_proj, w_o, mlp_norm, w_up, w_down):
    depth = w_in.shape[0]
    o = 0
    parts = []
    for n in (MLA_Q_RANK, MLA_KV_RANK, MLA_ROPE, 512, 512, 512, 2 * D_MODEL):
        parts.append(w_in[:, :, o:o + n])
        o += n
    w_qa, w_kva, w_kr, w_dq, w_dk, w_dv, w_gates = parts
    zl = jnp.zeros((depth, D_MODEL, MLA_NOPE), F32)
    zr = jnp.zeros((depth, D_MODEL, SLOT - MLA_NOPE - MLA_ROPE), F32)
    hr = MLA_ROPE // 2
    kr_slot = jnp.concatenate([zl, w_kr, zr], axis=-1)
    krsw_slot = jnp.concatenate([zl, w_kr[..., hr:], w_kr[..., :hr], zr], axis=-1)
    w1 = jnp.concatenate([w_qa, w_kva, kr_slot, krsw_slot], axis=-1).astype(BF16)

    wq = w_q_up.reshape(depth, MLA_Q_RANK, MLA_HEADS, MLA_NOPE + MLA_ROPE)
    qn, qr = wq[..., :MLA_NOPE], wq[..., MLA_NOPE:]
    z32 = jnp.zeros((depth, MLA_Q_RANK, MLA_HEADS, SLOT - MLA_NOPE - MLA_ROPE), F32)
    z64 = jnp.zeros((depth, MLA_Q_RANK, MLA_HEADS, MLA_NOPE), F32)
    raw = jnp.concatenate([qn, qr, z32], axis=-1).reshape(depth, MLA_Q_RANK, MLA_HEADS * SLOT)
    sw = jnp.concatenate([z64, qr[..., hr:], qr[..., :hr], z32], axis=-1).reshape(
        depth, MLA_Q_RANK, MLA_HEADS * SLOT)
    wqT = jnp.swapaxes(jnp.concatenate([raw, sw], axis=-1), 1, 2).astype(BF16)

    wkv = w_kv_up.reshape(depth, MLA_KV_RANK, MLA_HEADS, MLA_NOPE + MLA_V)
    kn, vv = wkv[..., :MLA_NOPE], wkv[..., MLA_NOPE:]
    wk = jnp.concatenate([kn, jnp.zeros_like(kn)], axis=-1).reshape(
        depth, MLA_KV_RANK, MLA_HEADS * SLOT).astype(BF16)
    wvT = jnp.swapaxes(jnp.pad(vv, ((0, 0), (0, 0), (0, 0), (0, V_EXTRA_ROWS))).reshape(
        depth, MLA_KV_RANK, MLA_HEADS * MLA_VROWS), 1, 2).astype(BF16)

    wdvT = jnp.swapaxes(jnp.pad(w_dv.reshape(depth, D_MODEL, DIFF_HEADS, DIFF_V),
                                ((0, 0), (0, 0), (0, 0), (0, V_EXTRA_ROWS))
                                ).reshape(depth, D_MODEL, DIFF_HEADS * DIFF_VROWS), 1, 2).astype(BF16)
    return {
        "g_attn": attn_norm[:, None, :], "w1": w1,
        "gq": mla_q_norm[:, None, :], "gkv": mla_kv_norm[:, None, :],
        "wqT": wqT, "wk": wk, "wvT": wvT,
        "wdqT": jnp.swapaxes(w_dq, 1, 2).astype(BF16), "wdk": w_dk.astype(BF16), "wdvT": wdvT,
        "wg": w_gates.astype(BF16), "bg": b_gate.reshape(depth, 1, 2 * D_MODEL),
        "wa": w_a_proj.astype(BF16), "wb": w_b_proj.astype(BF16), "wo": w_o.astype(BF16),
        "g_mlp": mlp_norm[:, None, :], "wup": w_up.astype(BF16), "wdn": w_down.astype(BF16),
    }


def kernel(x, meta_tokens, attn_norm, w_in, b_gate, mla_q_norm, w_q_up, mla_kv_norm, w_kv_up, lambda_q1, lambda_k1, lambda_q2, lambda_k2, diff_subln, w_a_proj, w_b_proj, w_o, mlp_norm, w_up, w_down, final_norm):
    B, seq, D = x.shape
    depth = attn_norm.shape[0]
    assert D == D_MODEL and seq % TM == 0 and B * CH == TM and meta_tokens.shape[0] == N_META
    n_real_tiles = seq // CH
    n_real_rows = B * seq

    meta_blk = jnp.pad(meta_tokens.astype(x.dtype), ((0, CH - N_META), (0, 0)))
    xa = x.reshape(n_real_rows, D)
    xb = jnp.concatenate([meta_blk] * B, axis=0)
    R = n_real_rows + B * CH
    tabs = _tables(B, seq)
    W = _prep_weights(attn_norm, w_in, b_gate, mla_q_norm, w_q_up, mla_kv_norm, w_kv_up,
                      w_a_proj, w_b_proj, w_o, mlp_norm, w_up, w_down)
    lam_params = jnp.stack([lambda_q1, lambda_k1, lambda_q2, lambda_k2], axis=1)
    subln_col = diff_subln[:, :, None]
    g_final = final_norm[None, :]

    for l in range(depth):
        qT, km, vT, dqT, dkm, dvT, gate, kn2 = _proj_call(xa, xb, R, l, W, tabs)
        aT = _mla_call(qT, km, vT, B, n_real_tiles)
        lam_init = 0.8 - 0.6 * math.exp(-0.3 * l)
        knorm = _key_norm_table(kn2, B, seq // TM)
        dT = _diff_call(l, lam_params, subln_col, knorm, dqT, dkm, tabs["alibi"], dvT, B, n_real_tiles, lam_init)
        last = l == depth - 1
        n_tiles = (n_real_rows // TM) if last else (R // TM)
        xa = _out_call(xa, xb, R, aT, dT, gate, l, W, g_final, n_tiles, last)
        xb = xa

    return xa.reshape(B, seq, D)
```

```python
import functools
import math

import jax
import jax.numpy as jnp
from jax import lax
from jax.experimental import pallas as pl
from jax.experimental.pallas import tpu as pltpu

D_MODEL = 1024
N_META = 16
EPS = 1e-6
MLA_HEADS = 8
MLA_Q_RANK = 256
MLA_KV_RANK = 128
MLA_NOPE = 64
MLA_ROPE = 32
MLA_V = 64
ROPE_BASE = 10000.0
DIFF_HEADS = 4
DIFF_HD = 64
DIFF_V = 128
D_FF = 4 * D_MODEL

LOG2E = 1.4426950408889634
NEG = -1e30

CH = 256
TK = 2 * CH
MK = N_META
KEY_TILES_PER_BLOCK = 2
SUM_LIMIT = 2.0 ** 100
SUM_FLOOR = 2.0 ** -40
SKIP_MARGIN = 100.0
NORM_SLACK = 1.05
DECAY_HEADS = 2
MLA_HEADS_PER_STEP = 8
DIFF_HEADS_PER_STEP = 4
TM = 512
SLOT = 128
V_EXTRA_ROWS = 16
MLA_VROWS = MLA_V + V_EXTRA_ROWS
DIFF_VROWS = DIFF_V + V_EXTRA_ROWS
VMEM_LIMIT = 56 * 1024 * 1024

F32 = jnp.float32
BF16 = jnp.bfloat16


def _rms(x, g):
    return x * lax.rsqrt(jnp.mean(x * x, axis=-1, keepdims=True) + EPS) * g


def _dot(a, b):
    return jnp.dot(a, b, preferred_element_type=F32)


def _dot_nt(a, b):
    return lax.dot_general(a, b, (((1,), (1,)), ((), ())), preferred_element_type=F32)


def _dot_tn(a, b):
    return lax.dot_general(a, b, (((0,), (0,)), ((), ())), preferred_element_type=F32)


def _pick_rows(xa_ref, xb_ref, tail):
    if tail is None:
        return xa_ref[...]
    return jnp.where(pl.program_id(0) == tail, xb_ref[...], xa_ref[...])


def _row_sources(xa, xb):
    na = xa.shape[0] // TM
    ib = xb.shape[0] // TM - 1
    return [pl.BlockSpec((TM, D_MODEL), lambda t: (jnp.minimum(t, na - 1), 0)),
            pl.BlockSpec((TM, D_MODEL), lambda t: (ib, 0))]


def _layer_spec(arr, l):
    nd = arr.ndim
    return pl.BlockSpec((None,) + arr.shape[1:], lambda *_: (l,) + (0,) * (nd - 1))


def _proj_kernel(xa_ref, xb_ref, g_ref, w1_ref, gq_ref, gkv_ref, wqT_ref, wk_ref, wvT_ref, onev_ref,
                 cqT_ref, sqT_ref, ck_ref, sk_ref, wdqT_ref, wdk_ref, wdvT_ref, onedv_ref,
                 wg_ref, bg_ref, grp_ref,
                 qT_out, km_out, vT_out, dqT_out, dkm_out, dvT_out, gate_out, kn_out, *, tail):
    x = _pick_rows(xa_ref, xb_ref, tail)
    hb = _rms(x, g_ref[...]).astype(BF16)

    p1 = _dot(hb, w1_ref[...])
    qlat = _rms(p1[:, :MLA_Q_RANK], gq_ref[...]).astype(BF16)
    kvlat = _rms(p1[:, MLA_Q_RANK:MLA_Q_RANK + MLA_KV_RANK], gkv_ref[...]).astype(BF16)
    kr = p1[:, 384:512]
    krsw = p1[:, 512:640]

    qT = _dot_nt(wqT_ref[...], qlat)
    cq = cqT_ref[...]
    sq = sqT_ref[...]
    half = MLA_HEADS * SLOT
    for h in range(MLA_HEADS):
        r = (qT[h * SLOT:(h + 1) * SLOT] * cq + qT[half + h * SLOT:half + (h + 1) * SLOT] * sq).astype(BF16)
        for c in range(TM // CH):
            qT_out[c, h * SLOT:(h + 1) * SLOT, :] = r[:, c * CH:(c + 1) * CH]

    krot = kr * ck_ref[...] + krsw * sk_ref[...]
    kn = _dot(kvlat, wk_ref[...])
    for h in range(MLA_HEADS):
        km_out[:, h * SLOT:(h + 1) * SLOT] = (kn[:, h * SLOT:(h + 1) * SLOT] + krot).astype(BF16)

    vT = (_dot_nt(wvT_ref[...], kvlat) + onev_ref[...]).astype(BF16)
    for c in range(TM // CH):
        vT_out[c] = vT[:, c * CH:(c + 1) * CH]

    dqT = (_dot_nt(wdqT_ref[...], hb) * (DIFF_HD ** -0.5 * LOG2E)).astype(BF16)
    for c in range(TM // CH):
        dqT_out[c] = dqT[:, c * CH:(c + 1) * CH]
    dk = _dot(hb, wdk_ref[...]).astype(BF16)
    dkm_out[...] = dk
    dkf = dk.astype(F32)
    n2 = _dot((dkf * dkf).astype(BF16), grp_ref[...])
    kn_out[...] = jnp.broadcast_to(jnp.max(n2, axis=0, keepdims=True), kn_out.shape)
    dvT = (_dot_nt(wdvT_ref[...], hb) + onedv_ref[...]).astype(BF16)
    for c in range(TM // CH):
        dvT_out[c] = dvT[:, c * CH:(c + 1) * CH]

    gate_out[...] = jax.nn.sigmoid(_dot(hb, wg_ref[...]) + bg_ref[...]).astype(BF16)


def _const_spec(shape):
    nd = len(shape)
    return pl.BlockSpec(shape, lambda *_: (0,) * nd)


def _proj_call(xa, xb, R, l, W, tabs):
    nt = R // TM
    nch = R // CH
    per = TM // CH
    n_real = nt - 1
    tiles_per_batch = tabs["ck"].shape[0] // TM - 1

    def tab_idx(t):
        return jnp.where(t < n_real, t % tiles_per_batch, tiles_per_batch)

    row = lambda w: pl.BlockSpec((TM, w), lambda t: (t, 0))
    row_tab = pl.BlockSpec((TM, SLOT), lambda t: (tab_idx(t), 0))
    col_tab = pl.BlockSpec((SLOT, TM), lambda t: (0, tab_idx(t)))
    chunk = lambda f: pl.BlockSpec((per, f, CH), lambda t: (t, 0, 0))
    lay = lambda name: (W[name], _layer_spec(W[name], l))
    xa_spec, xb_spec = _row_sources(xa, xb)
    ins = [
        (xa, xa_spec), (xb, xb_spec),
        lay("g_attn"), lay("w1"), lay("gq"), lay("gkv"), lay("wqT"), lay("wk"), lay("wvT"),
        (tabs["onev"], _const_spec(tabs["onev"].shape)),
        (tabs["cqT"], col_tab), (tabs["sqT"], col_tab), (tabs["ck"], row_tab), (tabs["sk"], row_tab),
        lay("wdqT"), lay("wdk"), lay("wdvT"),
        (tabs["onedv"], _const_spec(tabs["onedv"].shape)),
        lay("wg"), lay("bg"),
        (tabs["groups"], _const_spec(tabs["groups"].shape)),
    ]
    out_shape = [
        jax.ShapeDtypeStruct((nch, MLA_HEADS * SLOT, CH), BF16),
        jax.ShapeDtypeStruct((R, MLA_HEADS * SLOT), BF16),
        jax.ShapeDtypeStruct((nch, MLA_HEADS * MLA_VROWS, CH), BF16),
        jax.ShapeDtypeStruct((nch, DIFF_HEADS * SLOT, CH), BF16),
        jax.ShapeDtypeStruct((R, DIFF_HEADS * SLOT), BF16),
        jax.ShapeDtypeStruct((nch, DIFF_HEADS * DIFF_VROWS, CH), BF16),
        jax.ShapeDtypeStruct((R, 2 * D_MODEL), BF16),
        jax.ShapeDtypeStruct((nt, 8, SLOT), F32),
    ]
    out_specs = [
        chunk(MLA_HEADS * SLOT),
        row(MLA_HEADS * SLOT),
        chunk(MLA_HEADS * MLA_VROWS),
        chunk(DIFF_HEADS * SLOT),
        row(DIFF_HEADS * SLOT),
        chunk(DIFF_HEADS * DIFF_VROWS),
        row(2 * D_MODEL),
        pl.BlockSpec((None, 8, SLOT), lambda t: (t, 0, 0)),
    ]
    return pl.pallas_call(
        functools.partial(_proj_kernel, tail=None if xa.shape[0] == R else n_real),
        grid=(nt,),
        in_specs=[s for _, s in ins],
        out_specs=out_specs,
        out_shape=out_shape,
        compiler_params=pltpu.CompilerParams(
            dimension_semantics=("arbitrary",), vmem_limit_bytes=VMEM_LIMIT),
        name="proj",
    )(*[a for a, _ in ins])


def _tile_masks(i, n_real_tiles):
    is_meta = i == n_real_tiles
    big = 4 * TK
    krow = lax.broadcasted_iota(jnp.int32, (MK, CH), 0)
    qlane = lax.broadcasted_iota(jnp.int32, (MK, CH), 1)
    meta_mask = (krow < N_META) & (krow <= qlane + jnp.where(is_meta, 0, big))
    krow = lax.broadcasted_iota(jnp.int32, (TK, CH), 0)
    qlane = lax.broadcasted_iota(jnp.int32, (TK, CH), 1)
    last_mask = krow + jnp.where(is_meta, big, -(i % (TK // CH)) * CH) <= qlane
    return meta_mask, last_mask


def _col_max(s):
    while s.shape[0] > 32 and s.shape[0] % 4 == 0:
        r = s.shape[0] // 4
        s = jnp.maximum(jnp.maximum(s[:r], s[r:2 * r]), jnp.maximum(s[2 * r:3 * r], s[3 * r:]))
    return jnp.max(s, axis=0, keepdims=True)


def _softmax_steps(ss, ms, accs, vs, mask=None):
    if mask is not None:
        ss = [jnp.where(mask, s, NEG) for s in ss]
    m_new = [jnp.maximum(m, _col_max(s)) for s, m in zip(ss, ms)]
    ps = [jnp.exp2(s - m).astype(BF16) for s, m in zip(ss, m_new)]
    pvs = [_dot(v, p) for v, p in zip(vs, ps)]
    accs = [acc * jnp.exp2(m - mn) + pv for acc, m, mn, pv in zip(accs, ms, m_new, pvs)]
    return m_new, accs


def _attn_scratch(nc, vrows):
    return [pltpu.VMEM((nc, 1, CH), F32),
            pltpu.VMEM((nc, vrows, CH), F32)]


def _attn_core(i, n_real_tiles, qs, slopes, kmetas, vmetas, k_tiles, v_tiles, scr, finalize, decay=None):
    m_scr, acc_scr = scr
    cs = range(len(qs))
    kb = KEY_TILES_PER_BLOCK
    lrow = acc_scr.shape[1] - V_EXTRA_ROWS
    is_meta = i == n_real_tiles
    n = jnp.where(is_meta, 0, i // (TK // CH))
    n_groups = n // kb
    meta_mask, last_mask = _tile_masks(i, n_real_tiles)
    lane = lax.broadcasted_iota(jnp.int32, (1, CH), 1)
    ahead = ((i * CH + 1 + lane) * jnp.where(is_meta, 0, 1)).astype(F32)

    def run(tiles, lead, act=None, fresh=False, bias_after_lead=False):
        act = list(cs) if act is None else act
        if fresh:
            ms = [jnp.full((1, CH), NEG, F32) for _ in act]
            accs = [jnp.zeros(acc_scr.shape[1:], F32) for _ in act]
        else:
            ms = [m_scr[c] for c in act]
            accs = [acc_scr[c] for c in act]
        sss = [[_dot(ks[c], qs[c]) for c in act] for ks, _, _ in tiles]
        for idx, (ss, (_, vs, mask)) in enumerate(zip(sss, tiles)):
            vs = [vs[c] for c in act]
            if idx < lead:
                ms, accs = _softmax_steps(ss, ms, accs, vs, mask)
                if bias_after_lead and idx == lead - 1:
                    for j, c in enumerate(act):
                        if slopes[c]:
                            off = slopes[c] * ahead
                            accs[j] = accs[j] * jnp.exp2(-off)
                            ms[j] = ms[j] + off
            else:
                if mask is not None:
                    ss = [jnp.where(mask, s, NEG) for s in ss]
                ps = [jnp.exp2(s - m).astype(BF16) for s, m in zip(ss, ms)]
                pvs = [_dot(v, p) for v, p in zip(vs, ps)]
                accs = [acc + pv for acc, pv in zip(accs, pvs)]
        for j, c in enumerate(act):
            m_scr[c] = ms[j]
            acc_scr[c] = accs[j]

    def full(t):
        return (k_tiles(t), v_tiles(t), None)

    def first_tiles(r, odd):
        diag = (k_tiles(n), v_tiles(n), last_mask) if odd else (k_tiles(n, CH), v_tiles(n, CH), last_mask[:CH])
        return [(kmetas(), vmetas(), meta_mask), diag] + [full(n - 1 - u) for u in range(r)]

    def group_tiles(g):
        base = (n_groups - 1 - g) * kb
        return [full(base + kb - 1 - u) for u in range(kb)]

    def far_limit(chains):
        knorm_ref = decay[0]
        a = (i * CH + 1 + lane * 0).astype(F32)
        need = None
        for c in chains:
            qn = jnp.sqrt(jnp.sum(jnp.square(qs[c][:SLOT].astype(F32)), axis=0, keepdims=True))
            mm = m_scr[c] - slopes[c] * ahead
            nc = (qn * NORM_SLACK * knorm_ref[c:c + 1, :] - mm + SKIP_MARGIN) * (1.0 / slopes[c]) + N_META
            need = nc if need is None else jnp.maximum(need, nc)
        g = jnp.max(jnp.floor((need - a) * (1.0 / (kb * TK))), axis=1, keepdims=True)
        return jnp.clip(g.astype(jnp.int32)[0, 0] + n_groups + 1, 0, n_groups)

    def sweep(fixed):
        for r in range(kb):
            for odd in range(TK // CH):
                @pl.when((n % kb == r) & (i % (TK // CH) == odd))
                def _(r=r, odd=odd):
                    run(first_tiles(r, odd), 1 if fixed else 2 + r, fresh=True, bias_after_lead=fixed)

        def group(g, carry):
            run(group_tiles(g), 0 if fixed else kb)
            return carry

        if fixed and decay is not None:
            limits = [far_limit(chains) for chains in decay[1]]
            act, start = list(cs), 0
            for chains, limit in zip(decay[1], limits):
                stop = jnp.maximum(limit, start)

                def part(g, carry, act=tuple(act)):
                    run(group_tiles(g), 0, act=list(act))
                    return carry

                lax.fori_loop(start, stop, part, 0)
                act, start = [c for c in act if c not in chains], stop

            def rest(g, carry, act=tuple(act)):
                run(group_tiles(g), 0, act=list(act))
                return carry

            lax.fori_loop(start, n_groups, rest, 0)
        else:
            lax.fori_loop(0, n_groups, group, 0)

    sweep(fixed=True)
    sums = [acc_scr[c][lrow:lrow + 1] for c in cs]
    unsafe = [jnp.where((s > SUM_FLOOR) & (s < SUM_LIMIT), 0.0, 1.0) for s in sums]

    @pl.when(jnp.max(functools.reduce(jnp.maximum, unsafe)) > 0.5)
    def _():
        sweep(fixed=False)

    finalize([acc_scr[c] for c in cs])


def _mla_kernel(qT_ref, k_ref, kmeta_ref, vT_ref, vmeta_ref, o_ref, *scr, n_real_tiles):
    i = pl.program_id(2)
    heads = qT_ref.shape[0] // SLOT
    hs = range(heads)
    qs = [qT_ref[h * SLOT:(h + 1) * SLOT, :] for h in hs]

    def k_tiles(t, keys=TK):
        rows = pl.ds(pl.multiple_of(t * TK, TK), keys)
        return [k_ref[rows, h * SLOT:(h + 1) * SLOT] for h in hs]

    def v_tiles(t, keys=TK):
        return [jnp.concatenate([vT_ref[(TK // CH) * t + c, h * MLA_VROWS:(h + 1) * MLA_VROWS, :]
                                 for c in range(keys // CH)], axis=1) for h in hs]

    def kmetas():
        return [kmeta_ref[:MK, h * SLOT:(h + 1) * SLOT] for h in hs]

    def vmetas():
        return [vmeta_ref[h * MLA_VROWS:(h + 1) * MLA_VROWS, :MK] for h in hs]

    def finalize(accs):
        for h in hs:
            o = accs[h][:MLA_V] / accs[h][MLA_V:MLA_V + 1]
            o_ref[h * MLA_V:(h + 1) * MLA_V, :] = o.astype(o_ref.dtype)

    _attn_core(i, n_real_tiles, qs, [0.0] * heads, kmetas, vmetas, k_tiles, v_tiles, scr, finalize)


def _mla_call(qT, km, vT, B, n_real_tiles):
    hp = MLA_HEADS_PER_STEP
    ngrp = MLA_HEADS // hp
    nch = qT.shape[0]
    R = km.shape[0]
    seq = n_real_tiles * CH
    meta_chunk0 = B * n_real_tiles

    def q_idx(b, g, i):
        return (jnp.where(i == n_real_tiles, meta_chunk0 + b, b * n_real_tiles + i), g, 0)

    in_specs = [
        pl.BlockSpec((None, hp * SLOT, CH), q_idx),
        pl.BlockSpec((seq, hp * SLOT), lambda b, g, i: (b, g), pipeline_mode=pl.Buffered(1)),
        pl.BlockSpec((CH, hp * SLOT), lambda b, g, i: (meta_chunk0 + b, g)),
        pl.BlockSpec((n_real_tiles, hp * MLA_VROWS, CH), lambda b, g, i: (b, g, 0), pipeline_mode=pl.Buffered(1)),
        pl.BlockSpec((None, hp * MLA_VROWS, CH), lambda b, g, i: (meta_chunk0 + b, g, 0)),
    ]
    out_spec = pl.BlockSpec((None, hp * MLA_V, CH), q_idx)
    return pl.pallas_call(
        functools.partial(_mla_kernel, n_real_tiles=n_real_tiles),
        grid=(B, ngrp, n_real_tiles + 1),
        in_specs=in_specs,
        out_specs=out_spec,
        out_shape=jax.ShapeDtypeStruct((nch, MLA_HEADS * MLA_V, CH), BF16),
        scratch_shapes=_attn_scratch(hp, MLA_VROWS),
        compiler_params=pltpu.CompilerParams(
            dimension_semantics=("arbitrary", "arbitrary", "arbitrary"), vmem_limit_bytes=VMEM_LIMIT),
        name="mla_attn",
    )(qT, km, km, vT, vT)


def _diff_kernel(lam_ref, subln_ref, knorm_ref, qT_ref, k_ref, kmeta_ref, al_ref, almeta_ref, vT_ref, vmeta_ref,
                 o_ref, *scr, n_real_tiles, lam_init):
    i = pl.program_id(2)
    heads = qT_ref.shape[0] // SLOT

    rowi = lax.broadcasted_iota(jnp.int32, (SLOT, CH), 0)
    ext = jnp.where(rowi < 3, 1.0, 0.0).astype(BF16)
    qs = []
    for h in range(heads):
        q = qT_ref[h * SLOT:(h + 1) * SLOT, :]
        zero = jnp.zeros_like(q)
        qs.append(jnp.concatenate([jnp.where(rowi < DIFF_HD, q, zero), ext], axis=0))
        qs.append(jnp.concatenate([jnp.where(rowi >= DIFF_HD, q, zero), ext], axis=0))

    def per_map(xs):
        return [x for x in xs for _ in range(2)]

    def k_tiles(t, keys=TK):
        rows = pl.ds(pl.multiple_of(t * TK, TK), keys)
        return per_map([jnp.concatenate([k_ref[rows, h * SLOT:(h + 1) * SLOT], al_ref[h, rows, :]], axis=1)
                        for h in range(heads)])

    def v_tiles(t, keys=TK):
        return per_map([jnp.concatenate([vT_ref[(TK // CH) * t + c, h * DIFF_VROWS:(h + 1) * DIFF_VROWS, :]
                                         for c in range(keys // CH)], axis=1) for h in range(heads)])

    def kmetas():
        return per_map([jnp.concatenate([kmeta_ref[:MK, h * SLOT:(h + 1) * SLOT], almeta_ref[h, :MK, :]], axis=1)
                        for h in range(heads)])

    def vmetas():
        return per_map([vmeta_ref[h * DIFF_VROWS:(h + 1) * DIFF_VROWS, :MK] for h in range(heads)])

    def finalize(accs):
        lp = lam_ref[...]
        lam = (jnp.exp(jnp.sum(lp[0:1] * lp[1:2], axis=-1, keepdims=True))
               - jnp.exp(jnp.sum(lp[2:3] * lp[3:4], axis=-1, keepdims=True)) + lam_init)
        for h in range(heads):
            outs = [accs[c][:DIFF_V] / accs[c][DIFF_V:DIFF_V + 1] for c in (2 * h, 2 * h + 1)]
            a = outs[0] - lam * outs[1]
            a = a * lax.rsqrt(jnp.mean(a * a, axis=0, keepdims=True) + EPS) * subln_ref[...] * (1.0 - lam_init)
            o_ref[h * DIFF_V:(h + 1) * DIFF_V, :] = a.astype(o_ref.dtype)

    assert heads == DIFF_HEADS
    slopes = per_map([2.0 ** (-8.0 * (h + 1) / DIFF_HEADS) * LOG2E for h in range(heads)])
    _attn_core(i, n_real_tiles, qs, slopes, kmetas, vmetas, k_tiles, v_tiles, scr, finalize,
               decay=(knorm_ref, [[2 * h, 2 * h + 1] for h in range(DECAY_HEADS)]))


def _diff_call(l, lam_params, subln_col, knorm, dqT, dkm, alibi, dvT, B, n_real_tiles, lam_init):
    nch = dqT.shape[0]
    seq = n_real_tiles * CH
    meta_chunk0 = B * n_real_tiles

    hp = DIFF_HEADS_PER_STEP

    def q_idx(b, g, i):
        return (jnp.where(i == n_real_tiles, meta_chunk0 + b, b * n_real_tiles + i), g, 0)

    in_specs = [
        _layer_spec(lam_params, l),
        _layer_spec(subln_col, l),
        pl.BlockSpec((None,) + knorm.shape[1:], lambda b, g, i: (b, 0, 0)),
        pl.BlockSpec((None, hp * SLOT, CH), q_idx),
        pl.BlockSpec((seq, hp * SLOT), lambda b, g, i: (b, g), pipeline_mode=pl.Buffered(1)),
        pl.BlockSpec((CH, hp * SLOT), lambda b, g, i: (meta_chunk0 + b, g)),
        pl.BlockSpec((hp, seq, SLOT), lambda b, g, i: (g, 0, 0), pipeline_mode=pl.Buffered(1)),
        pl.BlockSpec((hp, CH, SLOT), lambda b, g, i: (g, seq // CH, 0)),
        pl.BlockSpec((n_real_tiles, hp * DIFF_VROWS, CH), lambda b, g, i: (b, g, 0), pipeline_mode=pl.Buffered(1)),
        pl.BlockSpec((None, hp * DIFF_VROWS, CH), lambda b, g, i: (meta_chunk0 + b, g, 0)),
    ]
    out_spec = pl.BlockSpec((None, hp * DIFF_V, CH), q_idx)
    return pl.pallas_call(
        functools.partial(_diff_kernel, n_real_tiles=n_real_tiles, lam_init=lam_init),
        grid=(B, DIFF_HEADS // hp, n_real_tiles + 1),
        in_specs=in_specs,
        out_specs=out_spec,
        out_shape=jax.ShapeDtypeStruct((nch, DIFF_HEADS * DIFF_V, CH), BF16),
        scratch_shapes=_attn_scratch(2 * hp, DIFF_VROWS),
        compiler_params=pltpu.CompilerParams(
            dimension_semantics=("arbitrary", "arbitrary", "arbitrary"), vmem_limit_bytes=VMEM_LIMIT),
        name="diff_attn",
    )(lam_params, subln_col, knorm, dqT, dkm, dkm, alibi, alibi, dvT, dvT)


def _out_kernel(xa_ref, xb_ref, aT_ref, dT_ref, gate_ref, wa_ref, wb_ref, wo_ref, gm_ref, wup_ref, wdn_ref,
                gf_ref, o_ref, *, final, tail):
    per = TM // CH
    ya = jnp.concatenate([_dot_tn(aT_ref[c], wa_ref[...]) for c in range(per)], axis=0)
    yb = jnp.concatenate([_dot_tn(dT_ref[c], wb_ref[...]) for c in range(per)], axis=0)
    gate = gate_ref[...].astype(F32)
    y = gate[:, :D_MODEL] * ya + gate[:, D_MODEL:] * yb
    x1 = _pick_rows(xa_ref, xb_ref, tail) + _dot(y.astype(BF16), wo_ref[...])
    h2 = _rms(x1, gm_ref[...]).astype(BF16)
    up = _dot(h2, wup_ref[...])
    act = jnp.square(jnp.maximum(up, 0.0)).astype(BF16)
    x2 = x1 + _dot(act, wdn_ref[...])
    if final:
        x2 = _rms(x2, gf_ref[...])
    o_ref[...] = x2


def _out_call(xa, xb, R, aT, dT, gate, l, W, g_final, n_tiles, final):
    per = TM // CH
    row = lambda w: pl.BlockSpec((TM, w), lambda t: (t, 0))
    chunk = lambda f: pl.BlockSpec((per, f, CH), lambda t: (t, 0, 0))
    names = ["wa", "wb", "wo", "g_mlp", "wup", "wdn"]
    in_specs = (_row_sources(xa, xb)
                + [chunk(MLA_HEADS * MLA_V), chunk(DIFF_HEADS * DIFF_V), row(2 * D_MODEL)]
                + [_layer_spec(W[k], l) for k in names]
                + [_const_spec((1, D_MODEL))])
    return pl.pallas_call(
        functools.partial(_out_kernel, final=final, tail=None if xa.shape[0] == R else R // TM - 1),
        grid=(n_tiles,),
        in_specs=in_specs,
        out_specs=row(D_MODEL),
        out_shape=jax.ShapeDtypeStruct((n_tiles * TM, D_MODEL), F32),
        compiler_params=pltpu.CompilerParams(
            dimension_semantics=("arbitrary",), vmem_limit_bytes=VMEM_LIMIT),
        name="out_mlp",
    )(xa, xb, aT, dT, gate, *[W[k] for k in names], g_final)


def _top16(v):
    bits = lax.bitcast_convert_type(v, jnp.uint32) & jnp.uint32(0xFFFF0000)
    return lax.bitcast_convert_type(bits, F32)


def _tables(B, seq):
    pos_real = N_META + jnp.arange(seq, dtype=F32)
    pos_meta = jnp.arange(CH, dtype=F32)
    pos = jnp.concatenate([pos_real] + [pos_meta] * (TM // CH))
    inv = 1.0 / (ROPE_BASE ** (jnp.arange(0, MLA_ROPE, 2, dtype=F32) / MLA_ROPE))
    ang = inv[:, None] * pos[None, :]
    cos, sin = jnp.cos(ang), jnp.sin(ang)
    n = pos.shape[0]
    ones = jnp.ones((MLA_NOPE, n), F32)
    zeros_n = jnp.zeros((MLA_NOPE, n), F32)
    pad = jnp.zeros((SLOT - MLA_NOPE - MLA_ROPE, n), F32)
    ckT = jnp.concatenate([ones, cos, cos, pad], axis=0)
    skT = jnp.concatenate([zeros_n, -sin, sin, pad], axis=0)
    cq = (MLA_NOPE + MLA_ROPE) ** -0.5 * LOG2E
    tabs = {"ck": ckT.T, "sk": skT.T, "cqT": ckT * cq, "sqT": skT * cq}

    slopes = 2.0 ** (-8.0 * jnp.arange(1, DIFF_HEADS + 1, dtype=F32) / DIFF_HEADS)
    pos1 = jnp.concatenate([pos_real, pos_meta])
    c = (slopes[:, None] * LOG2E) * pos1[None, :]
    c_hi = _top16(c)
    c_mid = _top16(c - c_hi)
    c_lo = c - c_hi - c_mid
    al = jnp.stack([c_hi, c_mid, c_lo], axis=-1).astype(BF16)
    tabs["alibi"] = jnp.pad(al, ((0, 0), (0, 0), (0, SLOT - 3)))

    onev = (jnp.arange(MLA_HEADS * MLA_VROWS) % MLA_VROWS == MLA_V).astype(F32)[:, None]
    onedv = (jnp.arange(DIFF_HEADS * DIFF_VROWS) % DIFF_VROWS == DIFF_V).astype(F32)[:, None]
    tabs["onev"] = onev
    tabs["onedv"] = onedv
    tabs["groups"] = (jnp.arange(DIFF_HEADS * SLOT)[:, None] // DIFF_HD == jnp.arange(SLOT)[None, :]).astype(BF16)
    return tabs


def _key_norm_table(kn2, B, tiles_per_batch):
    per_tile = kn2[:, 0, :2 * DIFF_HEADS]
    real = per_tile[:B * tiles_per_batch].reshape(B, tiles_per_batch, -1).max(axis=1)
    top = jnp.sqrt(jnp.maximum(real, per_tile[-1][None, :]))
    return jnp.broadcast_to(top[:, :, None], top.shape + (CH,))


def _prep_weights(attn_norm, w_in, b_gate, mla_q_norm, w_q_up, mla_kv_norm, w_kv_up,
                  w_a_proj, w_b_proj, w_o, mlp_norm, w_up, w_down):
    depth = w_in.shape[0]
    o = 0
    parts = []
    for n in (MLA_Q_RANK, MLA_KV_RANK, MLA_ROPE, 512, 512, 512, 2 * D_MODEL):
        parts.append(w_in[:, :, o:o + n])
        o += n
    w_qa, w_kva, w_kr, w_dq, w_dk, w_dv, w_gates = parts
    zl = jnp.zeros((depth, D_MODEL, MLA_NOPE), F32)
    zr = jnp.zeros((depth, D_MODEL, SLOT - MLA_NOPE - MLA_ROPE), F32)
    hr = MLA_ROPE // 2
    kr_slot = jnp.concatenate([zl, w_kr, zr], axis=-1)
    krsw_slot = jnp.concatenate([zl, w_kr[..., hr:], w_kr[..., :hr], zr], axis=-1)
    w1 = jnp.concatenate([w_qa, w_kva, kr_slot, krsw_slot], axis=-1).astype(BF16)

    wq = w_q_up.reshape(depth, MLA_Q_RANK, MLA_HEADS, MLA_NOPE + MLA_ROPE)
    qn, qr = wq[..., :MLA_NOPE], wq[..., MLA_NOPE:]
    z32 = jnp.zeros((depth, MLA_Q_RANK, MLA_HEADS, SLOT - MLA_NOPE - MLA_ROPE), F32)
    z64 = jnp.zeros((depth, MLA_Q_RANK, MLA_HEADS, MLA_NOPE), F32)
    raw = jnp.concatenate([qn, qr, z32], axis=-1).reshape(depth, MLA_Q_RANK, MLA_HEADS * SLOT)
    sw = jnp.concatenate([z64, qr[..., hr:], qr[..., :hr], z32], axis=-1).reshape(
        depth, MLA_Q_RANK, MLA_HEADS * SLOT)
    wqT = jnp.swapaxes(jnp.concatenate([raw, sw], axis=-1), 1, 2).astype(BF16)

    wkv = w_kv_up.reshape(depth, MLA_KV_RANK, MLA_HEADS, MLA_NOPE + MLA_V)
    kn, vv = wkv[..., :MLA_NOPE], wkv[..., MLA_NOPE:]
    wk = jnp.concatenate([kn, jnp.zeros_like(kn)], axis=-1).reshape(
        depth, MLA_KV_RANK, MLA_HEADS * SLOT).astype(BF16)
    wvT = jnp.swapaxes(jnp.pad(vv, ((0, 0), (0, 0), (0, 0), (0, V_EXTRA_ROWS))).reshape(
        depth, MLA_KV_RANK, MLA_HEADS * MLA_VROWS), 1, 2).astype(BF16)

    wdvT = jnp.swapaxes(jnp.pad(w_dv.reshape(depth, D_MODEL, DIFF_HEADS, DIFF_V),
                                ((0, 0), (0, 0), (0, 0), (0, V_EXTRA_ROWS))
                                ).reshape(depth, D_MODEL, DIFF_HEADS * DIFF_VROWS), 1, 2).astype(BF16)
    return {
        "g_attn": attn_norm[:, None, :], "w1": w1,
        "gq": mla_q_norm[:, None, :], "gkv": mla_kv_norm[:, None, :],
        "wqT": wqT, "wk": wk, "wvT": wvT,
        "wdqT": jnp.swapaxes(w_dq, 1, 2).astype(BF16), "wdk": w_dk.astype(BF16), "wdvT": wdvT,
        "wg": w_gates.astype(BF16), "bg": b_gate.reshape(depth, 1, 2 * D_MODEL),
        "wa": w_a_proj.astype(BF16), "wb": w_b_proj.astype(BF16), "wo": w_o.astype(BF16),
        "g_mlp": mlp_norm[:, None, :], "wup": w_up.astype(BF16), "wdn": w_down.astype(BF16),
    }


def kernel(x, meta_tokens, attn_norm, w_in, b_gate, mla_q_norm, w_q_up, mla_kv_norm, w_kv_up, lambda_q1, lambda_k1, lambda_q2, lambda_k2, diff_subln, w_a_proj, w_b_proj, w_o, mlp_norm, w_up, w_down, final_norm):
    B, seq, D = x.shape
    depth = attn_norm.shape[0]
    assert D == D_MODEL and seq % TM == 0 and B * CH == TM and meta_tokens.shape[0] == N_META
    n_real_tiles = seq // CH
    n_real_rows = B * seq

    meta_blk = jnp.pad(meta_tokens.astype(x.dtype), ((0, CH - N_META), (0, 0)))
    xa = x.reshape(n_real_rows, D)
    xb = jnp.concatenate([meta_blk] * B, axis=0)
    R = n_real_rows + B * CH
    tabs = _tables(B, seq)
    W = _prep_weights(attn_norm, w_in, b_gate, mla_q_norm, w_q_up, mla_kv_norm, w_kv_up,
                      w_a_proj, w_b_proj, w_o, mlp_norm, w_up, w_down)
    lam_params = jnp.stack([lambda_q1, lambda_k1, lambda_q2, lambda_k2], axis=1)
    subln_col = diff_subln[:, :, None]
    g_final = final_norm[None, :]

    for l in range(depth):
        qT, km, vT, dqT, dkm, dvT, gate, kn2 = _proj_call(xa, xb, R, l, W, tabs)
        aT = _mla_call(qT, km, vT, B, n_real_tiles)
        lam_init = 0.8 - 0.6 * math.exp(-0.3 * l)
        knorm = _key_norm_table(kn2, B, seq // TM)
        dT = _diff_call(l, lam_params, subln_col, knorm, dqT, dkm, tabs["alibi"], dvT, B, n_real_tiles, lam_init)
        last = l == depth - 1
        n_tiles = (n_real_rows // TM) if last else (R // TM)
        xa = _out_call(xa, xb, R, aT, dT, gate, l, W, g_final, n_tiles, last)
        xb = xa

    return xa.reshape(B, seq, D)
```

```python
import functools
import math

import jax
import jax.numpy as jnp
from jax import lax
from jax.experimental import pallas as pl
from jax.experimental.pallas import tpu as pltpu

D_MODEL = 1024
N_META = 16
EPS = 1e-6
MLA_HEADS = 8
MLA_Q_RANK = 256
MLA_KV_RANK = 128
MLA_NOPE = 64
MLA_ROPE = 32
MLA_V = 64
ROPE_BASE = 10000.0
DIFF_HEADS = 4
DIFF_HD = 64
DIFF_V = 128
D_FF = 4 * D_MODEL

LOG2E = 1.4426950408889634
NEG = -1e30

CH = 256
TK = 2 * CH
MK = N_META
KEY_TILES_PER_BLOCK = 2
SUM_LIMIT = 2.0 ** 100
SUM_FLOOR = 2.0 ** -40
SKIP_MARGIN = 100.0
NORM_SLACK = 1.05
DECAY_HEADS = 2
MLA_HEADS_PER_STEP = 8
DIFF_HEADS_PER_STEP = 4
TM = 512
SLOT = 128
V_EXTRA_ROWS = 16
MLA_VROWS = MLA_V + V_EXTRA_ROWS
DIFF_VROWS = DIFF_V + V_EXTRA_ROWS
VMEM_LIMIT = 56 * 1024 * 1024

F32 = jnp.float32
BF16 = jnp.bfloat16


def _rms(x, g):
    return x * lax.rsqrt(jnp.mean(x * x, axis=-1, keepdims=True) + EPS) * g


def _dot(a, b):
    return jnp.dot(a, b, preferred_element_type=F32)


def _dot_nt(a, b):
    return lax.dot_general(a, b, (((1,), (1,)), ((), ())), preferred_element_type=F32)


def _dot_tn(a, b):
    return lax.dot_general(a, b, (((0,), (0,)), ((), ())), preferred_element_type=F32)


def _pick_rows(xa_ref, xb_ref, tail):
    if tail is None:
        return xa_ref[...]
    return jnp.where(pl.program_id(0) == tail, xb_ref[...], xa_ref[...])


def _row_sources(xa, xb):
    na = xa.shape[0] // TM
    ib = xb.shape[0] // TM - 1
    return [pl.BlockSpec((TM, D_MODEL), lambda t: (jnp.minimum(t, na - 1), 0)),
            pl.BlockSpec((TM, D_MODEL), lambda t: (ib, 0))]


def _layer_spec(arr, l):
    nd = arr.ndim
    return pl.BlockSpec((None,) + arr.shape[1:], lambda *_: (l,) + (0,) * (nd - 1))


def _proj_kernel(xa_ref, xb_ref, g_ref, w1_ref, gq_ref, gkv_ref, wqT_ref, wk_ref, wvT_ref, onev_ref,
                 cqT_ref, sqT_ref, ck_ref, sk_ref, wdqT_ref, wdk_ref, wdvT_ref, onedv_ref,
                 wg_ref, bg_ref, grp_ref,
                 qT_out, km_out, vT_out, dqT_out, dkm_out, dvT_out, gate_out, kn_out, *, tail):
    x = _pick_rows(xa_ref, xb_ref, tail)
    hb = _rms(x, g_ref[...]).astype(BF16)

    p1 = _dot(hb, w1_ref[...])
    qlat = _rms(p1[:, :MLA_Q_RANK], gq_ref[...]).astype(BF16)
    kvlat = _rms(p1[:, MLA_Q_RANK:MLA_Q_RANK + MLA_KV_RANK], gkv_ref[...]).astype(BF16)
    kr = p1[:, 384:512]
    krsw = p1[:, 512:640]

    qT = _dot_nt(wqT_ref[...], qlat)
    cq = cqT_ref[...]
    sq = sqT_ref[...]
    half = MLA_HEADS * SLOT
    for h in range(MLA_HEADS):
        r = (qT[h * SLOT:(h + 1) * SLOT] * cq + qT[half + h * SLOT:half + (h + 1) * SLOT] * sq).astype(BF16)
        for c in range(TM // CH):
            qT_out[c, h * SLOT:(h + 1) * SLOT, :] = r[:, c * CH:(c + 1) * CH]

    krot = kr * ck_ref[...] + krsw * sk_ref[...]
    kn = _dot(kvlat, wk_ref[...])
    for h in range(MLA_HEADS):
        km_out[:, h * SLOT:(h + 1) * SLOT] = (kn[:, h * SLOT:(h + 1) * SLOT] + krot).astype(BF16)

    vT = (_dot_nt(wvT_ref[...], kvlat) + onev_ref[...]).astype(BF16)
    for c in range(TM // CH):
        vT_out[c] = vT[:, c * CH:(c + 1) * CH]

    dqT = (_dot_nt(wdqT_ref[...], hb) * (DIFF_HD ** -0.5 * LOG2E)).astype(BF16)
    for c in range(TM // CH):
        dqT_out[c] = dqT[:, c * CH:(c + 1) * CH]
    dk = _dot(hb, wdk_ref[...]).astype(BF16)
    dkm_out[...] = dk
    dkf = dk.astype(F32)
    n2 = _dot((dkf * dkf).astype(BF16), grp_ref[...])
    kn_out[...] = jnp.broadcast_to(jnp.max(n2, axis=0, keepdims=True), kn_out.shape)
    dvT = (_dot_nt(wdvT_ref[...], hb) + onedv_ref[...]).astype(BF16)
    for c in range(TM // CH):
        dvT_out[c] = dvT[:, c * CH:(c + 1) * CH]

    gate_out[...] = jax.nn.sigmoid(_dot(hb, wg_ref[...]) + bg_ref[...]).astype(BF16)


def _const_spec(shape):
    nd = len(shape)
    return pl.BlockSpec(shape, lambda *_: (0,) * nd)


def _proj_call(xa, xb, R, l, W, tabs):
    nt = R // TM
    nch = R // CH
    per = TM // CH
    n_real = nt - 1
    tiles_per_batch = tabs["ck"].shape[0] // TM - 1

    def tab_idx(t):
        return jnp.where(t < n_real, t % tiles_per_batch, tiles_per_batch)

    row = lambda w: pl.BlockSpec((TM, w), lambda t: (t, 0))
    row_tab = pl.BlockSpec((TM, SLOT), lambda t: (tab_idx(t), 0))
    col_tab = pl.BlockSpec((SLOT, TM), lambda t: (0, tab_idx(t)))
    chunk = lambda f: pl.BlockSpec((per, f, CH), lambda t: (t, 0, 0))
    lay = lambda name: (W[name], _layer_spec(W[name], l))
    xa_spec, xb_spec = _row_sources(xa, xb)
    ins = [
        (xa, xa_spec), (xb, xb_spec),
        lay("g_attn"), lay("w1"), lay("gq"), lay("gkv"), lay("wqT"), lay("wk"), lay("wvT"),
        (tabs["onev"], _const_spec(tabs["onev"].shape)),
        (tabs["cqT"], col_tab), (tabs["sqT"], col_tab), (tabs["ck"], row_tab), (tabs["sk"], row_tab),
        lay("wdqT"), lay("wdk"), lay("wdvT"),
        (tabs["onedv"], _const_spec(tabs["onedv"].shape)),
        lay("wg"), lay("bg"),
        (tabs["groups"], _const_spec(tabs["groups"].shape)),
    ]
    out_shape = [
        jax.ShapeDtypeStruct((nch, MLA_HEADS * SLOT, CH), BF16),
        jax.ShapeDtypeStruct((R, MLA_HEADS * SLOT), BF16),
        jax.ShapeDtypeStruct((nch, MLA_HEADS * MLA_VROWS, CH), BF16),
        jax.ShapeDtypeStruct((nch, DIFF_HEADS * SLOT, CH), BF16),
        jax.ShapeDtypeStruct((R, DIFF_HEADS * SLOT), BF16),
        jax.ShapeDtypeStruct((nch, DIFF_HEADS * DIFF_VROWS, CH), BF16),
        jax.ShapeDtypeStruct((R, 2 * D_MODEL), BF16),
        jax.ShapeDtypeStruct((nt, 8, SLOT), F32),
    ]
    out_specs = [
        chunk(MLA_HEADS * SLOT),
        row(MLA_HEADS * SLOT),
        chunk(MLA_HEADS * MLA_VROWS),
        chunk(DIFF_HEADS * SLOT),
        row(DIFF_HEADS * SLOT),
        chunk(DIFF_HEADS * DIFF_VROWS),
        row(2 * D_MODEL),
        pl.BlockSpec((None, 8, SLOT), lambda t: (t, 0, 0)),
    ]
    return pl.pallas_call(
        functools.partial(_proj_kernel, tail=None if xa.shape[0] == R else n_real),
        grid=(nt,),
        in_specs=[s for _, s in ins],
        out_specs=out_specs,
        out_shape=out_shape,
        compiler_params=pltpu.CompilerParams(
            dimension_semantics=("arbitrary",), vmem_limit_bytes=VMEM_LIMIT),
        name="proj",
    )(*[a for a, _ in ins])


def _tile_masks(i, n_real_tiles):
    is_meta = i == n_real_tiles
    big = 4 * TK
    krow = lax.broadcasted_iota(jnp.int32, (MK, CH), 0)
    qlane = lax.broadcasted_iota(jnp.int32, (MK, CH), 1)
    meta_mask = (krow < N_META) & (krow <= qlane + jnp.where(is_meta, 0, big))
    krow = lax.broadcasted_iota(jnp.int32, (TK, CH), 0)
    qlane = lax.broadcasted_iota(jnp.int32, (TK, CH), 1)
    last_mask = krow + jnp.where(is_meta, big, -(i % (TK // CH)) * CH) <= qlane
    return meta_mask, last_mask


def _col_max(s):
    while s.shape[0] > 32 and s.shape[0] % 4 == 0:
        r = s.shape[0] // 4
        s = jnp.maximum(jnp.maximum(s[:r], s[r:2 * r]), jnp.maximum(s[2 * r:3 * r], s[3 * r:]))
    return jnp.max(s, axis=0, keepdims=True)


def _softmax_steps(ss, ms, accs, vs, mask=None):
    if mask is not None:
        ss = [jnp.where(mask, s, NEG) for s in ss]
    m_new = [jnp.maximum(m, _col_max(s)) for s, m in zip(ss, ms)]
    ps = [jnp.exp2(s - m).astype(BF16) for s, m in zip(ss, m_new)]
    pvs = [_dot(v, p) for v, p in zip(vs, ps)]
    accs = [acc * jnp.exp2(m - mn) + pv for acc, m, mn, pv in zip(accs, ms, m_new, pvs)]
    return m_new, accs


def _attn_scratch(nc, vrows):
    return [pltpu.VMEM((nc, 1, CH), F32),
            pltpu.VMEM((nc, vrows, CH), F32)]


def _attn_core(i, n_real_tiles, qs, slopes, kmetas, vmetas, k_tiles, v_tiles, scr, finalize, decay=None):
    m_scr, acc_scr = scr
    cs = range(len(qs))
    kb = KEY_TILES_PER_BLOCK
    lrow = acc_scr.shape[1] - V_EXTRA_ROWS
    is_meta = i == n_real_tiles
    n = jnp.where(is_meta, 0, i // (TK // CH))
    n_groups = n // kb
    meta_mask, last_mask = _tile_masks(i, n_real_tiles)
    lane = lax.broadcasted_iota(jnp.int32, (1, CH), 1)
    ahead = ((i * CH + 1 + lane) * jnp.where(is_meta, 0, 1)).astype(F32)

    def run(tiles, lead, act=None, fresh=False, bias_after_lead=False):
        act = list(cs) if act is None else act
        if fresh:
            ms = [jnp.full((1, CH), NEG, F32) for _ in act]
            accs = [jnp.zeros(acc_scr.shape[1:], F32) for _ in act]
        else:
            ms = [m_scr[c] for c in act]
            accs = [acc_scr[c] for c in act]
        sss = [[_dot(ks[c], qs[c]) for c in act] for ks, _, _ in tiles]
        for idx, (ss, (_, vs, mask)) in enumerate(zip(sss, tiles)):
            vs = [vs[c] for c in act]
            if idx < lead:
                ms, accs = _softmax_steps(ss, ms, accs, vs, mask)
                if bias_after_lead and idx == lead - 1:
                    for j, c in enumerate(act):
                        if slopes[c]:
                            off = slopes[c] * ahead
                            accs[j] = accs[j] * jnp.exp2(-off)
                            ms[j] = ms[j] + off
            else:
                if mask is not None:
                    ss = [jnp.where(mask, s, NEG) for s in ss]
                ps = [jnp.exp2(s - m).astype(BF16) for s, m in zip(ss, ms)]
                pvs = [_dot(v, p) for v, p in zip(vs, ps)]
                accs = [acc + pv for acc, pv in zip(accs, pvs)]
        for j, c in enumerate(act):
            m_scr[c] = ms[j]
            acc_scr[c] = accs[j]

    def full(t):
        return (k_tiles(t), v_tiles(t), None)

    def first_tiles(r, odd):
        diag = (k_tiles(n), v_tiles(n), last_mask) if odd else (k_tiles(n, CH), v_tiles(n, CH), last_mask[:CH])
        return [(kmetas(), vmetas(), meta_mask), diag] + [full(n - 1 - u) for u in range(r)]

    def group_tiles(g):
        base = (n_groups - 1 - g) * kb
        return [full(base + kb - 1 - u) for u in range(kb)]

    def far_limit(chains):
        knorm_ref = decay[0]
        a = (i * CH + 1 + lane * 0).astype(F32)
        need = None
        for c in chains:
            qn = jnp.sqrt(jnp.sum(jnp.square(qs[c][:SLOT].astype(F32)), axis=0, keepdims=True))
            mm = m_scr[c] - slopes[c] * ahead
            nc = (qn * NORM_SLACK * knorm_ref[c:c + 1, :] - mm + SKIP_MARGIN) * (1.0 / slopes[c]) + N_META
            need = nc if need is None else jnp.maximum(need, nc)
        g = jnp.max(jnp.floor((need - a) * (1.0 / (kb * TK))), axis=1, keepdims=True)
        return jnp.clip(g.astype(jnp.int32)[0, 0] + n_groups + 1, 0, n_groups)

    def sweep(fixed):
        for r in range(kb):
            for odd in range(TK // CH):
                @pl.when((n % kb == r) & (i % (TK // CH) == odd))
                def _(r=r, odd=odd):
                    run(first_tiles(r, odd), 1 if fixed else 2 + r, fresh=True, bias_after_lead=fixed)

        def group(g, carry):
            run(group_tiles(g), 0 if fixed else kb)
            return carry

        if fixed and decay is not None:
            limits = [far_limit(chains) for chains in decay[1]]
            act, start = list(cs), 0
            for chains, limit in zip(decay[1], limits):
                stop = jnp.maximum(limit, start)

                def part(g, carry, act=tuple(act)):
                    run(group_tiles(g), 0, act=list(act))
                    return carry

                lax.fori_loop(start, stop, part, 0)
                act, start = [c for c in act if c not in chains], stop

            def rest(g, carry, act=tuple(act)):
                run(group_tiles(g), 0, act=list(act))
                return carry

            lax.fori_loop(start, n_groups, rest, 0)
        else:
            lax.fori_loop(0, n_groups, group, 0)

    sweep(fixed=True)
    sums = [acc_scr[c][lrow:lrow + 1] for c in cs]
    unsafe = [jnp.where((s > SUM_FLOOR) & (s < SUM_LIMIT), 0.0, 1.0) for s in sums]

    @pl.when(jnp.max(functools.reduce(jnp.maximum, unsafe)) > 0.5)
    def _():
        sweep(fixed=False)

    finalize([acc_scr[c] for c in cs])


def _mla_kernel(qT_ref, k_ref, kmeta_ref, vT_ref, vmeta_ref, o_ref, *scr, n_real_tiles):
    i = pl.program_id(2)
    heads = qT_ref.shape[0] // SLOT
    hs = range(heads)
    qs = [qT_ref[h * SLOT:(h + 1) * SLOT, :] for h in hs]

    def k_tiles(t, keys=TK):
        rows = pl.ds(pl.multiple_of(t * TK, TK), keys)
        return [k_ref[rows, h * SLOT:(h + 1) * SLOT] for h in hs]

    def v_tiles(t, keys=TK):
        return [jnp.concatenate([vT_ref[(TK // CH) * t + c, h * MLA_VROWS:(h + 1) * MLA_VROWS, :]
                                 for c in range(keys // CH)], axis=1) for h in hs]

    def kmetas():
        return [kmeta_ref[:MK, h * SLOT:(h + 1) * SLOT] for h in hs]

    def vmetas():
        return [vmeta_ref[h * MLA_VROWS:(h + 1) * MLA_VROWS, :MK] for h in hs]

    def finalize(accs):
        for h in hs:
            o = accs[h][:MLA_V] / accs[h][MLA_V:MLA_V + 1]
            o_ref[h * MLA_V:(h + 1) * MLA_V, :] = o.astype(o_ref.dtype)

    _attn_core(i, n_real_tiles, qs, [0.0] * heads, kmetas, vmetas, k_tiles, v_tiles, scr, finalize)


def _mla_call(qT, km, vT, B, n_real_tiles):
    hp = MLA_HEADS_PER_STEP
    ngrp = MLA_HEADS // hp
    nch = qT.shape[0]
    R = km.shape[0]
    seq = n_real_tiles * CH
    meta_chunk0 = B * n_real_tiles

    def q_idx(b, g, i):
        return (jnp.where(i == n_real_tiles, meta_chunk0 + b, b * n_real_tiles + i), g, 0)

    in_specs = [
        pl.BlockSpec((None, hp * SLOT, CH), q_idx),
        pl.BlockSpec((seq, hp * SLOT), lambda b, g, i: (b, g), pipeline_mode=pl.Buffered(1)),
        pl.BlockSpec((CH, hp * SLOT), lambda b, g, i: (meta_chunk0 + b, g)),
        pl.BlockSpec((n_real_tiles, hp * MLA_VROWS, CH), lambda b, g, i: (b, g, 0)),
        pl.BlockSpec((None, hp * MLA_VROWS, CH), lambda b, g, i: (meta_chunk0 + b, g, 0)),
    ]
    out_spec = pl.BlockSpec((None, hp * MLA_V, CH), q_idx)
    return pl.pallas_call(
        functools.partial(_mla_kernel, n_real_tiles=n_real_tiles),
        grid=(B, ngrp, n_real_tiles + 1),
        in_specs=in_specs,
        out_specs=out_spec,
        out_shape=jax.ShapeDtypeStruct((nch, MLA_HEADS * MLA_V, CH), BF16),
        scratch_shapes=_attn_scratch(hp, MLA_VROWS),
        compiler_params=pltpu.CompilerParams(
            dimension_semantics=("arbitrary", "arbitrary", "arbitrary"), vmem_limit_bytes=VMEM_LIMIT),
        name="mla_attn",
    )(qT, km, km, vT, vT)


def _diff_kernel(lam_ref, subln_ref, knorm_ref, qT_ref, k_ref, kmeta_ref, al_ref, almeta_ref, vT_ref, vmeta_ref,
                 o_ref, *scr, n_real_tiles, lam_init):
    i = pl.program_id(2)
    heads = qT_ref.shape[0] // SLOT

    rowi = lax.broadcasted_iota(jnp.int32, (SLOT, CH), 0)
    ext = jnp.where(rowi < 3, 1.0, 0.0).astype(BF16)
    qs = []
    for h in range(heads):
        q = qT_ref[h * SLOT:(h + 1) * SLOT, :]
        zero = jnp.zeros_like(q)
        qs.append(jnp.concatenate([jnp.where(rowi < DIFF_HD, q, zero), ext], axis=0))
        qs.append(jnp.concatenate([jnp.where(rowi >= DIFF_HD, q, zero), ext], axis=0))

    def per_map(xs):
        return [x for x in xs for _ in range(2)]

    def k_tiles(t, keys=TK):
        rows = pl.ds(pl.multiple_of(t * TK, TK), keys)
        return per_map([jnp.concatenate([k_ref[rows, h * SLOT:(h + 1) * SLOT], al_ref[h, rows, :]], axis=1)
                        for h in range(heads)])

    def v_tiles(t, keys=TK):
        return per_map([jnp.concatenate([vT_ref[(TK // CH) * t + c, h * DIFF_VROWS:(h + 1) * DIFF_VROWS, :]
                                         for c in range(keys // CH)], axis=1) for h in range(heads)])

    def kmetas():
        return per_map([jnp.concatenate([kmeta_ref[:MK, h * SLOT:(h + 1) * SLOT], almeta_ref[h, :MK, :]], axis=1)
                        for h in range(heads)])

    def vmetas():
        return per_map([vmeta_ref[h * DIFF_VROWS:(h + 1) * DIFF_VROWS, :MK] for h in range(heads)])

    def finalize(accs):
        lp = lam_ref[...]
        lam = (jnp.exp(jnp.sum(lp[0:1] * lp[1:2], axis=-1, keepdims=True))
               - jnp.exp(jnp.sum(lp[2:3] * lp[3:4], axis=-1, keepdims=True)) + lam_init)
        for h in range(heads):
            outs = [accs[c][:DIFF_V] / accs[c][DIFF_V:DIFF_V + 1] for c in (2 * h, 2 * h + 1)]
            a = outs[0] - lam * outs[1]
            a = a * lax.rsqrt(jnp.mean(a * a, axis=0, keepdims=True) + EPS) * subln_ref[...] * (1.0 - lam_init)
            o_ref[h * DIFF_V:(h + 1) * DIFF_V, :] = a.astype(o_ref.dtype)

    assert heads == DIFF_HEADS
    slopes = per_map([2.0 ** (-8.0 * (h + 1) / DIFF_HEADS) * LOG2E for h in range(heads)])
    _attn_core(i, n_real_tiles, qs, slopes, kmetas, vmetas, k_tiles, v_tiles, scr, finalize,
               decay=(knorm_ref, [[2 * h, 2 * h + 1] for h in range(DECAY_HEADS)]))


def _diff_call(l, lam_params, subln_col, knorm, dqT, dkm, alibi, dvT, B, n_real_tiles, lam_init):
    nch = dqT.shape[0]
    seq = n_real_tiles * CH
    meta_chunk0 = B * n_real_tiles

    hp = DIFF_HEADS_PER_STEP

    def q_idx(b, g, i):
        return (jnp.where(i == n_real_tiles, meta_chunk0 + b, b * n_real_tiles + i), g, 0)

    in_specs = [
        _layer_spec(lam_params, l),
        _layer_spec(subln_col, l),
        pl.BlockSpec((None,) + knorm.shape[1:], lambda b, g, i: (b, 0, 0)),
        pl.BlockSpec((None, hp * SLOT, CH), q_idx),
        pl.BlockSpec((seq, hp * SLOT), lambda b, g, i: (b, g)),
        pl.BlockSpec((CH, hp * SLOT), lambda b, g, i: (meta_chunk0 + b, g)),
        pl.BlockSpec((hp, seq, SLOT), lambda b, g, i: (g, 0, 0), pipeline_mode=pl.Buffered(1)),
        pl.BlockSpec((hp, CH, SLOT), lambda b, g, i: (g, seq // CH, 0)),
        pl.BlockSpec((n_real_tiles, hp * DIFF_VROWS, CH), lambda b, g, i: (b, g, 0), pipeline_mode=pl.Buffered(1)),
        pl.BlockSpec((None, hp * DIFF_VROWS, CH), lambda b, g, i: (meta_chunk0 + b, g, 0)),
    ]
    out_spec = pl.BlockSpec((None, hp * DIFF_V, CH), q_idx)
    return pl.pallas_call(
        functools.partial(_diff_kernel, n_real_tiles=n_real_tiles, lam_init=lam_init),
        grid=(B, DIFF_HEADS // hp, n_real_tiles + 1),
        in_specs=in_specs,
        out_specs=out_spec,
        out_shape=jax.ShapeDtypeStruct((nch, DIFF_HEADS * DIFF_V, CH), BF16),
        scratch_shapes=_attn_scratch(2 * hp, DIFF_VROWS),
        compiler_params=pltpu.CompilerParams(
            dimension_semantics=("arbitrary", "arbitrary", "arbitrary"), vmem_limit_bytes=VMEM_LIMIT),
        name="diff_attn",
    )(lam_params, subln_col, knorm, dqT, dkm, dkm, alibi, alibi, dvT, dvT)


def _out_kernel(xa_ref, xb_ref, aT_ref, dT_ref, gate_ref, wa_ref, wb_ref, wo_ref, gm_ref, wup_ref, wdn_ref,
                gf_ref, o_ref, *, final, tail):
    per = TM // CH
    ya = jnp.concatenate([_dot_tn(aT_ref[c], wa_ref[...]) for c in range(per)], axis=0)
    yb = jnp.concatenate([_dot_tn(dT_ref[c], wb_ref[...]) for c in range(per)], axis=0)
    gate = gate_ref[...].astype(F32)
    y = gate[:, :D_MODEL] * ya + gate[:, D_MODEL:] * yb
    x1 = _pick_rows(xa_ref, xb_ref, tail) + _dot(y.astype(BF16), wo_ref[...])
    h2 = _rms(x1, gm_ref[...]).astype(BF16)
    up = _dot(h2, wup_ref[...])
    act = jnp.square(jnp.maximum(up, 0.0)).astype(BF16)
    x2 = x1 + _dot(act, wdn_ref[...])
    if final:
        x2 = _rms(x2, gf_ref[...])
    o_ref[...] = x2


def _out_call(xa, xb, R, aT, dT, gate, l, W, g_final, n_tiles, final):
    per = TM // CH
    row = lambda w: pl.BlockSpec((TM, w), lambda t: (t, 0))
    chunk = lambda f: pl.BlockSpec((per, f, CH), lambda t: (t, 0, 0))
    names = ["wa", "wb", "wo", "g_mlp", "wup", "wdn"]
    in_specs = (_row_sources(xa, xb)
                + [chunk(MLA_HEADS * MLA_V), chunk(DIFF_HEADS * DIFF_V), row(2 * D_MODEL)]
                + [_layer_spec(W[k], l) for k in names]
                + [_const_spec((1, D_MODEL))])
    return pl.pallas_call(
        functools.partial(_out_kernel, final=final, tail=None if xa.shape[0] == R else R // TM - 1),
        grid=(n_tiles,),
        in_specs=in_specs,
        out_specs=row(D_MODEL),
        out_shape=jax.ShapeDtypeStruct((n_tiles * TM, D_MODEL), F32),
        compiler_params=pltpu.CompilerParams(
            dimension_semantics=("arbitrary",), vmem_limit_bytes=VMEM_LIMIT),
        name="out_mlp",
    )(xa, xb, aT, dT, gate, *[W[k] for k in names], g_final)


def _top16(v):
    bits = lax.bitcast_convert_type(v, jnp.uint32) & jnp.uint32(0xFFFF0000)
    return lax.bitcast_convert_type(bits, F32)


def _tables(B, seq):
    pos_real = N_META + jnp.arange(seq, dtype=F32)
    pos_meta = jnp.arange(CH, dtype=F32)
    pos = jnp.concatenate([pos_real] + [pos_meta] * (TM // CH))
    inv = 1.0 / (ROPE_BASE ** (jnp.arange(0, MLA_ROPE, 2, dtype=F32) / MLA_ROPE))
    ang = inv[:, None] * pos[None, :]
    cos, sin = jnp.cos(ang), jnp.sin(ang)
    n = pos.shape[0]
    ones = jnp.ones((MLA_NOPE, n), F32)
    zeros_n = jnp.zeros((MLA_NOPE, n), F32)
    pad = jnp.zeros((SLOT - MLA_NOPE - MLA_ROPE, n), F32)
    ckT = jnp.concatenate([ones, cos, cos, pad], axis=0)
    skT = jnp.concatenate([zeros_n, -sin, sin, pad], axis=0)
    cq = (MLA_NOPE + MLA_ROPE) ** -0.5 * LOG2E
    tabs = {"ck": ckT.T, "sk": skT.T, "cqT": ckT * cq, "sqT": skT * cq}

    slopes = 2.0 ** (-8.0 * jnp.arange(1, DIFF_HEADS + 1, dtype=F32) / DIFF_HEADS)
    pos1 = jnp.concatenate([pos_real, pos_meta])
    c = (slopes[:, None] * LOG2E) * pos1[None, :]
    c_hi = _top16(c)
    c_mid = _top16(c - c_hi)
    c_lo = c - c_hi - c_mid
    al = jnp.stack([c_hi, c_mid, c_lo], axis=-1).astype(BF16)
    tabs["alibi"] = jnp.pad(al, ((0, 0), (0, 0), (0, SLOT - 3)))

    onev = (jnp.arange(MLA_HEADS * MLA_VROWS) % MLA_VROWS == MLA_V).astype(F32)[:, None]
    onedv = (jnp.arange(DIFF_HEADS * DIFF_VROWS) % DIFF_VROWS == DIFF_V).astype(F32)[:, None]
    tabs["onev"] = onev
    tabs["onedv"] = onedv
    tabs["groups"] = (jnp.arange(DIFF_HEADS * SLOT)[:, None] // DIFF_HD == jnp.arange(SLOT)[None, :]).astype(BF16)
    return tabs


def _key_norm_table(kn2, B, tiles_per_batch):
    per_tile = kn2[:, 0, :2 * DIFF_HEADS]
    real = per_tile[:B * tiles_per_batch].reshape(B, tiles_per_batch, -1).max(axis=1)
    top = jnp.sqrt(jnp.maximum(real, per_tile[-1][None, :]))
    return jnp.broadcast_to(top[:, :, None], top.shape + (CH,))


def _prep_weights(attn_norm, w_in, b_gate, mla_q_norm, w_q_up, mla_kv_norm, w_kv_up,
                  w_a_proj, w_b_proj, w_o, mlp_norm, w_up, w_down):
    depth = w_in.shape[0]
    o = 0
    parts = []
    for n in (MLA_Q_RANK, MLA_KV_RANK, MLA_ROPE, 512, 512, 512, 2 * D_MODEL):
        parts.append(w_in[:, :, o:o + n])
        o += n
    w_qa, w_kva, w_kr, w_dq, w_dk, w_dv, w_gates = parts
    zl = jnp.zeros((depth, D_MODEL, MLA_NOPE), F32)
    zr = jnp.zeros((depth, D_MODEL, SLOT - MLA_NOPE - MLA_ROPE), F32)
    hr = MLA_ROPE // 2
    kr_slot = jnp.concatenate([zl, w_kr, zr], axis=-1)
    krsw_slot = jnp.concatenate([zl, w_kr[..., hr:], w_kr[..., :hr], zr], axis=-1)
    w1 = jnp.concatenate([w_qa, w_kva, kr_slot, krsw_slot], axis=-1).astype(BF16)

    wq = w_q_up.reshape(depth, MLA_Q_RANK, MLA_HEADS, MLA_NOPE + MLA_ROPE)
    qn, qr = wq[..., :MLA_NOPE], wq[..., MLA_NOPE:]
    z32 = jnp.zeros((depth, MLA_Q_RANK, MLA_HEADS, SLOT - MLA_NOPE - MLA_ROPE), F32)
    z64 = jnp.zeros((depth, MLA_Q_RANK, MLA_HEADS, MLA_NOPE), F32)
    raw = jnp.concatenate([qn, qr, z32], axis=-1).reshape(depth, MLA_Q_RANK, MLA_HEADS * SLOT)
    sw = jnp.concatenate([z64, qr[..., hr:], qr[..., :hr], z32], axis=-1).reshape(
        depth, MLA_Q_RANK, MLA_HEADS * SLOT)
    wqT = jnp.swapaxes(jnp.concatenate([raw, sw], axis=-1), 1, 2).astype(BF16)

    wkv = w_kv_up.reshape(depth, MLA_KV_RANK, MLA_HEADS, MLA_NOPE + MLA_V)
    kn, vv = wkv[..., :MLA_NOPE], wkv[..., MLA_NOPE:]
    wk = jnp.concatenate([kn, jnp.zeros_like(kn)], axis=-1).reshape(
        depth, MLA_KV_RANK, MLA_HEADS * SLOT).astype(BF16)
    wvT = jnp.swapaxes(jnp.pad(vv, ((0, 0), (0, 0), (0, 0), (0, V_EXTRA_ROWS))).reshape(
        depth, MLA_KV_RANK, MLA_HEADS * MLA_VROWS), 1, 2).astype(BF16)

    wdvT = jnp.swapaxes(jnp.pad(w_dv.reshape(depth, D_MODEL, DIFF_HEADS, DIFF_V),
                                ((0, 0), (0, 0), (0, 0), (0, V_EXTRA_ROWS))
                                ).reshape(depth, D_MODEL, DIFF_HEADS * DIFF_VROWS), 1, 2).astype(BF16)
    return {
        "g_attn": attn_norm[:, None, :], "w1": w1,
        "gq": mla_q_norm[:, None, :], "gkv": mla_kv_norm[:, None, :],
        "wqT": wqT, "wk": wk, "wvT": wvT,
        "wdqT": jnp.swapaxes(w_dq, 1, 2).astype(BF16), "wdk": w_dk.astype(BF16), "wdvT": wdvT,
        "wg": w_gates.astype(BF16), "bg": b_gate.reshape(depth, 1, 2 * D_MODEL),
        "wa": w_a_proj.astype(BF16), "wb": w_b_proj.astype(BF16), "wo": w_o.astype(BF16),
        "g_mlp": mlp_norm[:, None, :], "wup": w_up.astype(BF16), "wdn": w_down.astype(BF16),
    }


def kernel(x, meta_tokens, attn_norm, w_in, b_gate, mla_q_norm, w_q_up, mla_kv_norm, w_kv_up, lambda_q1, lambda_k1, lambda_q2, lambda_k2, diff_subln, w_a_proj, w_b_proj, w_o, mlp_norm, w_up, w_down, final_norm):
    B, seq, D = x.shape
    depth = attn_norm.shape[0]
    assert D == D_MODEL and seq % TM == 0 and B * CH == TM and meta_tokens.shape[0] == N_META
    n_real_tiles = seq // CH
    n_real_rows = B * seq

    meta_blk = jnp.pad(meta_tokens.astype(x.dtype), ((0, CH - N_META), (0, 0)))
    xa = x.reshape(n_real_rows, D)
    xb = jnp.concatenate([meta_blk] * B, axis=0)
    R = n_real_rows + B * CH
    tabs = _tables(B, seq)
    W = _prep_weights(attn_norm, w_in, b_gate, mla_q_norm, w_q_up, mla_kv_norm, w_kv_up,
                      w_a_proj, w_b_proj, w_o, mlp_norm, w_up, w_down)
    lam_params = jnp.stack([lambda_q1, lambda_k1, lambda_q2, lambda_k2], axis=1)
    subln_col = diff_subln[:, :, None]
    g_final = final_norm[None, :]

    for l in range(depth):
        qT, km, vT, dqT, dkm, dvT, gate, kn2 = _proj_call(xa, xb, R, l, W, tabs)
        aT = _mla_call(qT, km, vT, B, n_real_tiles)
        lam_init = 0.8 - 0.6 * math.exp(-0.3 * l)
        knorm = _key_norm_table(kn2, B, seq // TM)
        dT = _diff_call(l, lam_params, subln_col, knorm, dqT, dkm, tabs["alibi"], dvT, B, n_real_tiles, lam_init)
        last = l == depth - 1
        n_tiles = (n_real_rows // TM) if last else (R // TM)
        xa = _out_call(xa, xb, R, aT, dT, gate, l, W, g_final, n_tiles, last)
        xb = xa

    return xa.reshape(B, seq, D)
```
